```python
import jax, jax.numpy as jnp
from jax import lax
import numpy as np

D_MODEL = 1024
BATCH = 8
SEQ = 4096
DEPTH = 2

D_FF = 2816
CHUNK = 128
A_GROUPS = 4
A_GROUP_CH = 128
A_HALF = A_GROUPS * A_GROUP_CH
N_HEADS = 8
HEAD_DIM = 64
ATT_W = N_HEADS * HEAD_DIM
IDX_HEADS = 8
IDX_DIM = 64
TOPK_MAX = 256
Q_BLOCK = 128
ROPE_THETA = 10000.0
EPS = 1e-6

SPLIT_SIZES = (A_HALF, A_HALF, ATT_W, ATT_W, ATT_W, IDX_HEADS * IDX_DIM, IDX_DIM, IDX_HEADS, D_MODEL, D_MODEL)
SPLIT_POINTS = tuple(int(v) for v in np.cumsum(SPLIT_SIZES)[:-1])
N_IN = int(sum(SPLIT_SIZES))

kernel_name = "hybrid_gmlp_dsa_macaron"


def _rms_norm(x, g):
    xf = x.astype(jnp.float32)
    y = xf * lax.rsqrt(jnp.mean(xf * xf, axis=-1, keepdims=True) + EPS)
    return (y * g.astype(jnp.float32)).astype(x.dtype)


def _layer_norm(x, g, b):
    xf = x.astype(jnp.float32)
    mu = jnp.mean(xf, axis=-1, keepdims=True)
    var = jnp.mean(jnp.square(xf - mu), axis=-1, keepdims=True)
    y = (xf - mu) * lax.rsqrt(var + EPS)
    return (y * g.astype(jnp.float32) + b.astype(jnp.float32)).astype(x.dtype)


def _swiglu(h, w_gu, w_down):
    gate, up = jnp.split(h @ w_gu, 2, axis=-1)
    return (jax.nn.silu(gate) * up) @ w_down


def _rope(x, pos):
    d = x.shape[-1]
    inv = ROPE_THETA ** (-jnp.arange(0, d, 2, dtype=jnp.float32) / d)
    ang = pos.astype(jnp.float32)[:, None] * inv[None, :]
    cos = jnp.cos(ang)[None, :, None, :].astype(x.dtype)
    sin = jnp.sin(ang)[None, :, None, :].astype(x.dtype)
    x1, x2 = x[..., : d // 2], x[..., d // 2:]
    return jnp.concatenate([x1 * cos - x2 * sin, x2 * cos + x1 * sin], axis=-1)


def _gmlp_spatial(u, v, ln_g, ln_b, w_s, b_s):
    B, T, _ = u.shape
    u = jax.nn.gelu(u)
    v = _layer_norm(jax.nn.gelu(v), ln_g, ln_b)
    vb = v.reshape(B, T // CHUNK, CHUNK, A_GROUPS, A_GROUP_CH)
    mask = jnp.tril(jnp.ones((CHUNK, CHUNK), dtype=bool))
    ws = jnp.where(mask[None], w_s, jnp.zeros((), w_s.dtype))
    mixed = jnp.einsum('gts,bnsgc->bntgc', ws, vb) + b_s.T[:, :, None]
    return u * mixed.reshape(B, T, A_HALF)


def _dsa_attention(q, k, v, q_idx, k_idx, w_idx):
    B, T, H, Dh = q.shape
    L = k.shape[1]
    topk = min(TOPK_MAX, L // 4)
    nb = T // Q_BLOCK
    idx_scale = (IDX_DIM ** -0.5) * (IDX_HEADS ** -0.5)
    att_scale = HEAD_DIM ** -0.5
    kpos = jnp.arange(L, dtype=jnp.int32)
    qpos = jnp.arange(T, dtype=jnp.int32).reshape(nb, Q_BLOCK)
    gather = jax.vmap(lambda a, i: a[i])

    def to_blocks(a):
        return a.reshape((B, nb, Q_BLOCK) + a.shape[2:]).swapaxes(0, 1)

    def one_block(args):
        qb, qib, wb, pb = args
        logits = jnp.einsum('bqhd,bsd->bqhs', qib, k_idx).astype(jnp.float32)
        score = jnp.einsum('bqhs,bqh->bqs', jax.nn.relu(logits), wb.astype(jnp.float32)) * idx_scale
        causal = kpos[None, :] <= pb[:, None]
        score = jnp.where(causal[None], score, -jnp.inf)
        _, sel = lax.top_k(score, topk)
        k_sel = gather(k, sel)
        v_sel = gather(v, sel)
        s = jnp.einsum('bqhd,bqkhd->bhqk', qb, k_sel).astype(jnp.float32) * att_scale
        valid = sel <= pb[None, :, None]
        s = jnp.where(valid[:, None], s, -jnp.inf)
        p = jax.nn.softmax(s, axis=-1).astype(v.dtype)
        return jnp.einsum('bhqk,bqkhd->bqhd', p, v_sel)

    out = lax.map(one_block, (to_blocks(q), to_blocks(q_idx), to_blocks(w_idx), qpos))
    return out.swapaxes(0, 1).reshape(B, T, H * Dh)


def _hybrid_mixer(h, w_in, b_gate, gmlp_ln_g, gmlp_ln_b, gmlp_w_s, gmlp_b_s, w_pa, w_pb, w_out):
    B, T, _ = h.shape
    proj = h @ w_in
    u, va, q, k, v, qi, ki, wi, g_a, g_b = jnp.split(proj, SPLIT_POINTS, axis=-1)
    pos = jnp.arange(T, dtype=jnp.int32)
    y_a = _gmlp_spatial(u, va, gmlp_ln_g, gmlp_ln_b, gmlp_w_s, gmlp_b_s) @ w_pa
    q = _rope(q.reshape(B, T, N_HEADS, HEAD_DIM), pos)
    k = _rope(k.reshape(B, T, N_HEADS, HEAD_DIM), pos)
    v = v.reshape(B, T, N_HEADS, HEAD_DIM)
    qi = _rope(qi.reshape(B, T, IDX_HEADS, IDX_DIM), pos)
    ki = _rope(ki[:, :, None, :], pos)[:, :, 0, :]
    y_b = _dsa_attention(q, k, v, qi, ki, wi) @ w_pb
    gates = jax.nn.sigmoid(jnp.concatenate([g_a, g_b], axis=-1) + b_gate)
    gate_a, gate_b = jnp.split(gates, 2, axis=-1)
    return (gate_a * y_a + gate_b * y_b) @ w_out


def setup_inputs(seed: int = 0) -> dict:
    key = jax.random.key(seed)
    ks = jax.random.split(key, 20)
    f32 = jnp.float32

    def nrm(k, shape, scale):
        return jax.random.normal(k, shape, f32) * scale

    def gain(k, shape):
        return 1.0 + 0.05 * jax.random.normal(k, shape, f32)

    return {
        "x": jax.random.normal(ks[0], (BATCH, SEQ, D_MODEL), f32),
        "ffn1_norm": gain(ks[1], (DEPTH, D_MODEL)),
        "ffn1_w_gu": nrm(ks[2], (DEPTH, D_MODEL, 2 * D_FF), D_MODEL ** -0.5),
        "ffn1_w_down": nrm(ks[3], (DEPTH, D_FF, D_MODEL), D_FF ** -0.5),
        "mix_norm": gain(ks[4], (DEPTH, D_MODEL)),
        "w_in": nrm(ks[5], (DEPTH, D_MODEL, N_IN), D_MODEL ** -0.5),
        "b_gate": nrm(ks[6], (DEPTH, 2 * D_MODEL), 0.02),
        "gmlp_ln_g": gain(ks[7], (DEPTH, A_HALF)),
        "gmlp_ln_b": nrm(ks[8], (DEPTH, A_HALF), 0.02),
        "gmlp_w_s": nrm(ks[9], (DEPTH, A_GROUPS, CHUNK, CHUNK), 0.5 * CHUNK ** -0.5),
        "gmlp_b_s": 1.0 + 0.1 * jax.random.normal(ks[10], (DEPTH, A_GROUPS, CHUNK), f32),
        "w_pa": nrm(ks[11], (DEPTH, A_HALF, D_MODEL), A_HALF ** -0.5),
        "w_pb": nrm(ks[12], (DEPTH, ATT_W, D_MODEL), ATT_W ** -0.5),
        "w_out": nrm(ks[13], (DEPTH, D_MODEL, D_MODEL), D_MODEL ** -0.5),
        "ffn2_norm": gain(ks[14], (DEPTH, D_MODEL)),
        "ffn2_w_gu": nrm(ks[15], (DEPTH, D_MODEL, 2 * D_FF), D_MODEL ** -0.5),
        "ffn2_w_down": nrm(ks[16], (DEPTH, D_FF, D_MODEL), D_FF ** -0.5),
        "final_norm": gain(ks[17], (D_MODEL,)),
    }


def reference(x, ffn1_norm, ffn1_w_gu, ffn1_w_down, mix_norm, w_in, b_gate, gmlp_ln_g, gmlp_ln_b,
              gmlp_w_s, gmlp_b_s, w_pa, w_pb, w_out, ffn2_norm, ffn2_w_gu, ffn2_w_down, final_norm):
    for l in range(DEPTH):
        x = x + 0.5 * _swiglu(_rms_norm(x, ffn1_norm[l]), ffn1_w_gu[l], ffn1_w_down[l])
        x = x + _hybrid_mixer(_rms_norm(x, mix_norm[l]), w_in[l], b_gate[l], gmlp_ln_g[l], gmlp_ln_b[l],
                              gmlp_w_s[l], gmlp_b_s[l], w_pa[l], w_pb[l], w_out[l])
        x = x + 0.5 * _swiglu(_rms_norm(x, ffn2_norm[l]), ffn2_w_gu[l], ffn2_w_down[l])
    return _rms_norm(x, final_norm)
```

```python
import functools

import jax
import jax.numpy as jnp
import numpy as np
from jax import lax
from jax.experimental import pallas as pl
from jax.experimental.pallas import tpu as pltpu

F32 = jnp.float32
BF16 = jnp.bfloat16

CHUNK = 128
A_GROUPS = 4
A_GROUP_CH = 128
A_HALF = A_GROUPS * A_GROUP_CH
N_HEADS = 8
HEAD_DIM = 64
ATT_W = N_HEADS * HEAD_DIM
IDX_HEADS = 8
IDX_DIM = 64
TOPK_MAX = 256
ROPE_THETA = 10000.0
EPS = 1e-6
IDX_SCALE = (IDX_DIM ** -0.5) * (IDX_HEADS ** -0.5)
ATT_SCALE = HEAD_DIM ** -0.5

LANES = 128
VMEM_LIMIT_BYTES = 56 * 1024 * 1024

FFN_TM = 512
FFN_TF_TILES = 2
ROW_TM = 512
ATT_BQ = 128
ATT_BK = 512

INT_MIN = np.int32(-2 ** 31)
F32_MAX = float(np.finfo(np.float32).max)


def _rms(x, g):
    return x * lax.rsqrt(jnp.mean(x * x, axis=-1, keepdims=True) + EPS) * g


def _cparams(n_axes):
    return pltpu.CompilerParams(
        dimension_semantics=("arbitrary",) * n_axes, vmem_limit_bytes=VMEM_LIMIT_BYTES)


def _ffn_kernel(x_ref, g_ref, wg_ref, wu_ref, wd_ref, *rest, n_tiles, final):
    if final:
        gf_ref, o_ref, hn_ref, acc_ref = rest
    else:
        o_ref, hn_ref, acc_ref = rest
    j = pl.program_id(1)

    @pl.when(j == 0)
    def _():
        hn_ref[...] = _rms(x_ref[...], g_ref[...]).astype(BF16)
        acc_ref[...] = jnp.zeros_like(acc_ref)

    hn = hn_ref[...]
    gate = jnp.dot(hn, wg_ref[...], preferred_element_type=F32)
    up = jnp.dot(hn, wu_ref[...], preferred_element_type=F32)
    a = (jax.nn.silu(gate) * up).astype(BF16)
    acc_ref[...] += jnp.dot(a, wd_ref[...], preferred_element_type=F32)

    @pl.when(j == n_tiles - 1)
    def _():
        y = x_ref[...] + 0.5 * acc_ref[...]
        if final:
            y = _rms(y, gf_ref[...])
        o_ref[...] = y


def _ffn(x, g, w_gu, w_down, final_g=None):
    n, d = x.shape
    f = w_down.shape[0]
    tm, nt = FFN_TM, FFN_TF_TILES
    tf = f // nt
    assert n % tm == 0 and f % nt == 0 and tf % LANES == 0
    final = final_g is not None
    in_specs = [
        pl.BlockSpec((tm, d), lambda i, j: (i, 0)),
        pl.BlockSpec((1, d), lambda i, j: (0, 0)),
        pl.BlockSpec((d, tf), lambda i, j: (0, j)),
        pl.BlockSpec((d, tf), lambda i, j: (0, j + nt)),
        pl.BlockSpec((tf, d), lambda i, j: (j, 0)),
    ]
    args = [x, g.reshape(1, d), w_gu, w_gu, w_down]
    if final:
        in_specs.append(pl.BlockSpec((1, d), lambda i, j: (0, 0)))
        args.append(final_g.reshape(1, d))
    return pl.pallas_call(
        functools.partial(_ffn_kernel, n_tiles=nt, final=final),
        grid=(n // tm, nt),
        in_specs=in_specs,
        out_specs=pl.BlockSpec((tm, d), lambda i, j: (i, 0)),
        out_shape=jax.ShapeDtypeStruct((n, d), F32),
        scratch_shapes=[pltpu.VMEM((tm, d), BF16), pltpu.VMEM((tm, d), F32)],
        compiler_params=_cparams(2),
        name="ffn",
    )(*args)


def _gmlp_kernel(x_ref, g_ref, wuv_ref, lng_ref, lnb_ref, ws_ref, bs_ref, o_ref, *, tm):
    h = _rms(x_ref[...], g_ref[...]).astype(BF16)
    uv = jnp.dot(h, wuv_ref[...], preferred_element_type=F32)
    u = jax.nn.gelu(uv[:, :A_HALF])
    v = jax.nn.gelu(uv[:, A_HALF:])
    mu = jnp.mean(v, axis=-1, keepdims=True)
    var = jnp.mean(jnp.square(v - mu), axis=-1, keepdims=True)
    v = ((v - mu) * lax.rsqrt(var + EPS) * lng_ref[...] + lnb_ref[...]).astype(BF16)
    tri = (lax.broadcasted_iota(jnp.int32, (CHUNK, CHUNK), 1)
           <= lax.broadcasted_iota(jnp.int32, (CHUNK, CHUNK), 0))
    for g in range(A_GROUPS):
        ws = jnp.where(tri, ws_ref[g], 0.0).astype(BF16)
        cols = slice(g * A_GROUP_CH, (g + 1) * A_GROUP_CH)
        for c in range(tm // CHUNK):
            rows = slice(c * CHUNK, (c + 1) * CHUNK)
            mixed = jnp.dot(ws, v[rows, cols], preferred_element_type=F32) + bs_ref[g]
            o_ref[rows, cols] = (u[rows, cols] * mixed).astype(BF16)


def _gmlp(x, g, w_uv, ln_g, ln_b, w_s, b_s):
    n, d = x.shape
    tm = ROW_TM
    assert n % tm == 0 and tm % CHUNK == 0
    bs_b = jnp.broadcast_to(b_s[:, :, None], (A_GROUPS, CHUNK, A_GROUP_CH))
    return pl.pallas_call(
        functools.partial(_gmlp_kernel, tm=tm),
        grid=(n // tm,),
        in_specs=[
            pl.BlockSpec((tm, d), lambda i: (i, 0)),
            pl.BlockSpec((1, d), lambda i: (0, 0)),
            pl.BlockSpec((d, 2 * A_HALF), lambda i: (0, 0)),
            pl.BlockSpec((1, A_HALF), lambda i: (0, 0)),
            pl.BlockSpec((1, A_HALF), lambda i: (0, 0)),
            pl.BlockSpec((A_GROUPS, CHUNK, CHUNK), lambda i: (0, 0, 0)),
            pl.BlockSpec((A_GROUPS, CHUNK, A_GROUP_CH), lambda i: (0, 0, 0)),
        ],
        out_specs=pl.BlockSpec((tm, A_HALF), lambda i: (i, 0)),
        out_shape=jax.ShapeDtypeStruct((n, A_HALF), BF16),
        compiler_params=_cparams(1),
        name="gmlp",
    )(x, g.reshape(1, d), w_uv, ln_g.reshape(1, A_HALF), ln_b.reshape(1, A_HALF), w_s, bs_b)


def _rope(x, cos2, sin2):
    lane = lax.broadcasted_iota(jnp.int32, (x.shape[0], LANES), 1)
    first_half = (lane % HEAD_DIM) < (HEAD_DIM // 2)
    outs = []
    for c in range(x.shape[1] // LANES):
        xb = x[:, c * LANES:(c + 1) * LANES]
        partner = jnp.where(first_half,
                            pltpu.roll(xb, LANES - HEAD_DIM // 2, axis=1),
                            pltpu.roll(xb, HEAD_DIM // 2, axis=1))
        outs.append(xb * cos2 + partner * sin2)
    return outs[0] if len(outs) == 1 else jnp.concatenate(outs, axis=1)


def _prep_kernel(x_ref, g_ref, wqkv_ref, wqi_ref, wkw_ref, cos_ref, sin_ref,
                 q_ref, k_ref, v_ref, qi_ref, ki_ref, wi_ref):
    h = _rms(x_ref[...], g_ref[...]).astype(BF16)
    cos2, sin2 = cos_ref[...], sin_ref[...]
    qkv = jnp.dot(h, wqkv_ref[...], preferred_element_type=F32)
    q_ref[...] = (_rope(qkv[:, :ATT_W], cos2, sin2) * ATT_SCALE).astype(BF16)
    k_ref[...] = _rope(qkv[:, ATT_W:2 * ATT_W], cos2, sin2).astype(BF16)
    v_ref[...] = qkv[:, 2 * ATT_W:].astype(BF16)
    qi = jnp.dot(h, wqi_ref[...], preferred_element_type=F32)
    qi_ref[...] = _rope(qi, cos2, sin2).astype(BF16)
    kw = jnp.dot(h, wkw_ref[...], preferred_element_type=F32)
    ki_ref[...] = _rope(kw, cos2, sin2)[:, :IDX_DIM].astype(BF16)
    wi_ref[...] = kw[:, IDX_DIM:IDX_DIM + IDX_HEADS]


def _prep(x, g, w_qkv, w_qi, w_kw, cos2, sin2, seq):
    n, d = x.shape
    tm = ROW_TM
    assert n % tm == 0 and seq % tm == 0
    tpb = seq // tm
    full = lambda i: (0, 0)
    row = lambda i: (i, 0)
    pos = lambda i: (i % tpb, 0)
    return pl.pallas_call(
        _prep_kernel,
        grid=(n // tm,),
        in_specs=[
            pl.BlockSpec((tm, d), row),
            pl.BlockSpec((1, d), full),
            pl.BlockSpec((d, 3 * ATT_W), full),
            pl.BlockSpec((d, IDX_HEADS * IDX_DIM), full),
            pl.BlockSpec((d, LANES), full),
            pl.BlockSpec((tm, LANES), pos),
            pl.BlockSpec((tm, LANES), pos),
        ],
        out_specs=[
            pl.BlockSpec((tm, ATT_W), row),
            pl.BlockSpec((tm, ATT_W), row),
            pl.BlockSpec((tm, ATT_W), row),
            pl.BlockSpec((tm, IDX_HEADS * IDX_DIM), row),
            pl.BlockSpec((tm, IDX_DIM), row),
            pl.BlockSpec((tm, IDX_HEADS), row),
        ],
        out_shape=[
            jax.ShapeDtypeStruct((n, ATT_W), BF16),
            jax.ShapeDtypeStruct((n, ATT_W), BF16),
            jax.ShapeDtypeStruct((n, ATT_W), BF16),
            jax.ShapeDtypeStruct((n, IDX_HEADS * IDX_DIM), BF16),
            jax.ShapeDtypeStruct((n, IDX_DIM), BF16),
            jax.ShapeDtypeStruct((n, IDX_HEADS), F32),
        ],
        compiler_params=_cparams(1),
        name="attn_prep",
    )(x, g.reshape(1, d), w_qkv, w_qi, w_kw, cos2, sin2)


def _lane_sum(x):
    return jnp.broadcast_to(jnp.sum(x, axis=1, keepdims=True), x.shape)


def _attn_kernel(q_ref, qi_ref, wi_ref, k_ref, v_ref, ki_ref, o_ref, sc_ref, s_ref, wb_ref,
                 *, bq, bk, topk, idx_bits):
    i = pl.program_id(1)
    nk = (i * bq + bq + bk - 1) // bk
    nlb = bk // LANES
    topk_f = float(topk)
    nt_dims = (((1,), (1,)), ((), ()))

    wi = wi_ref[0]
    for h in range(IDX_HEADS):
        wb_ref[h] = jnp.broadcast_to(wi[:, h:h + 1], (bq, LANES))

    qpos = i * bq + lax.broadcasted_iota(jnp.int32, (bq, bk), 0)
    kcol = lax.broadcasted_iota(jnp.int32, (bq, bk), 1)

    def idx_body(kc, carry):
        kib = ki_ref[0, pl.ds(pl.multiple_of(kc * bk, bk), bk), :]
        acc = jnp.zeros((bq, bk), F32)
        for h in range(IDX_HEADS):
            l = lax.dot_general(qi_ref[0, :, h * IDX_DIM:(h + 1) * IDX_DIM], kib, nt_dims,
                                preferred_element_type=F32)
            wb = wb_ref[h]
            acc = acc + jnp.maximum(l, 0.0) * jnp.concatenate([wb] * nlb, axis=1)
        sc = acc * IDX_SCALE
        sc_ref[kc] = jnp.where(kc * bk + kcol <= qpos, sc, -jnp.inf)
        return carry

    lax.fori_loop(0, nk, idx_body, 0)

    def reduce_rows(fn, init, combine):
        def body(kc, acc):
            s = sc_ref[kc]
            for c in range(nlb):
                acc = combine(acc, fn(s[:, c * LANES:(c + 1) * LANES], kc * bk + c * LANES))
            return acc
        return lax.fori_loop(0, nk, body, init)

    zeros = jnp.zeros((bq, LANES), F32)

    def count(pred):
        return _lane_sum(reduce_rows(lambda s, off: jnp.where(pred(s, off), 1.0, 0.0), zeros, jnp.add))

    def row_min(fn):
        part = reduce_rows(fn, jnp.full((bq, LANES), jnp.inf, F32), jnp.minimum)
        return jnp.broadcast_to(jnp.min(part, axis=1, keepdims=True), part.shape)

    def key_to_f32(u):
        key = u ^ INT_MIN
        bits = jnp.where(key < 0, key ^ np.int32(0x7FFFFFFF), key)
        return lax.bitcast_convert_type(bits, F32)

    def bis_body(it, u):
        trial = u | jnp.left_shift(jnp.int32(1), 31 - it)
        thr = key_to_f32(trial)
        cnt = count(lambda s, off: s >= thr)
        return jnp.where(cnt >= topk_f, trial, u)

    u = lax.fori_loop(0, 32, bis_body, jnp.zeros((bq, LANES), jnp.int32))
    thr0 = jnp.where(u == 0, -F32_MAX, key_to_f32(u))

    t = row_min(lambda s, off: jnp.where(s >= thr0, s, jnp.inf))
    c_gt = count(lambda s, off: s > t)

    def adv_body(st):
        t, c_gt, _ = st
        t_next = row_min(lambda s, off: jnp.where(s > t, s, jnp.inf))
        t = jnp.where(c_gt >= topk_f, t_next, t)
        c_gt = count(lambda s, off: s > t)
        return t, c_gt, jnp.max(c_gt)

    t, c_gt, _ = lax.while_loop(lambda st: st[2] >= topk_f, adv_body, (t, c_gt, jnp.max(c_gt)))

    c_ge = count(lambda s, off: s >= t)
    rem = topk_f - c_gt
    lane_id = lax.broadcasted_iota(jnp.int32, (bq, LANES), 1)

    def tie_split():
        def body(it, p):
            trial = p | jnp.left_shift(jnp.int32(1), idx_bits - 1 - it)
            cnt = count(lambda s, off: (s == t) & (off + lane_id < trial))
            return jnp.where(cnt < rem, trial, p)
        return lax.fori_loop(0, idx_bits, body, jnp.zeros((bq, LANES), jnp.int32))

    need_split = jnp.max(c_ge - c_gt - rem) > 0.0
    last_eq = lax.cond(need_split, tie_split, lambda: jnp.full((bq, LANES), 2 ** idx_bits, jnp.int32))

    def bias_body(kc, carry):
        s = sc_ref[kc]
        parts = []
        for c in range(nlb):
            sb = s[:, c * LANES:(c + 1) * LANES]
            pos = kc * bk + c * LANES + lane_id
            sel = (sb > t) | ((sb == t) & (pos <= last_eq))
            parts.append(jnp.where(sel, 0.0, -jnp.inf))
        sc_ref[kc] = jnp.concatenate(parts, axis=1)
        return carry

    lax.fori_loop(0, nk, bias_body, 0)

    for h in range(N_HEADS):
        hs = slice(h * HEAD_DIM, (h + 1) * HEAD_DIM)
        qh = q_ref[0, :, hs]

        def qk_body(kc, m):
            kh = k_ref[0, pl.ds(pl.multiple_of(kc * bk, bk), bk), hs]
            s = lax.dot_general(qh, kh, nt_dims, preferred_element_type=F32) + sc_ref[kc]
            s_ref[kc] = s
            for c in range(nlb):
                m = jnp.maximum(m, s[:, c * LANES:(c + 1) * LANES])
            return m

        m = lax.fori_loop(0, nk, qk_body, jnp.full((bq, LANES), -jnp.inf, F32))
        m = jnp.broadcast_to(jnp.max(m, axis=1, keepdims=True), (bq, LANES))
        m_full = jnp.concatenate([m] * nlb, axis=1)

        def pv_body(kc, st):
            l, acc = st
            p = jnp.exp(s_ref[kc] - m_full)
            for c in range(nlb):
                l = l + p[:, c * LANES:(c + 1) * LANES]
            vh = v_ref[0, pl.ds(pl.multiple_of(kc * bk, bk), bk), hs]
            acc = acc + jnp.dot(p.astype(BF16), vh, preferred_element_type=F32)
            return l, acc

        l, acc = lax.fori_loop(0, nk, pv_body, (zeros, jnp.zeros((bq, HEAD_DIM), F32)))
        o_ref[0, :, hs] = (acc / jnp.sum(l, axis=1, keepdims=True)).astype(BF16)


def _attention(q, qi, wi, k, v, ki):
    b, t, _ = q.shape
    bq, bk = ATT_BQ, ATT_BK
    assert t % bk == 0 and bk % bq == 0 and bq % LANES == 0
    topk = min(TOPK_MAX, t // 4)
    idx_bits = max(1, int(np.ceil(np.log2(t))))
    qspec = lambda w: pl.BlockSpec((1, bq, w), lambda bi, i: (bi, i, 0))
    kspec = lambda w: pl.BlockSpec((1, t, w), lambda bi, i: (bi, 0, 0))
    return pl.pallas_call(
        functools.partial(_attn_kernel, bq=bq, bk=bk, topk=topk, idx_bits=idx_bits),
        grid=(b, t // bq),
        in_specs=[qspec(ATT_W), qspec(IDX_HEADS * IDX_DIM), qspec(IDX_HEADS),
                  kspec(ATT_W), kspec(ATT_W), kspec(IDX_DIM)],
        out_specs=qspec(ATT_W),
        out_shape=jax.ShapeDtypeStruct((b, t, ATT_W), BF16),
        scratch_shapes=[
            pltpu.VMEM((t // bk, bq, bk), F32),
            pltpu.VMEM((t // bk, bq, bk), F32),
            pltpu.VMEM((IDX_HEADS, bq, LANES), F32),
        ],
        compiler_params=_cparams(2),
        name="dsa_attention",
    )(q, qi, wi, k, v, ki)


def _merge_kernel(x_ref, g_ref, wg_ref, bg_ref, ya_ref, yb_ref, wpa_ref, wpb_ref, wo_ref, o_ref):
    x = x_ref[...]
    d = x.shape[1]
    h = _rms(x, g_ref[...]).astype(BF16)
    gates = jax.nn.sigmoid(jnp.dot(h, wg_ref[...], preferred_element_type=F32) + bg_ref[...])
    ya = jnp.dot(ya_ref[...], wpa_ref[...], preferred_element_type=F32)
    yb = jnp.dot(yb_ref[...], wpb_ref[...], preferred_element_type=F32)
    m = (gates[:, :d] * ya + gates[:, d:] * yb).astype(BF16)
    o_ref[...] = x + jnp.dot(m, wo_ref[...], preferred_element_type=F32)


def _merge(x, g, w_g, b_gate, ya, yb, w_pa, w_pb, w_out):
    n, d = x.shape
    tm = ROW_TM
    assert n % tm == 0
    full = lambda i: (0, 0)
    row = lambda i: (i, 0)
    return pl.pallas_call(
        _merge_kernel,
        grid=(n // tm,),
        in_specs=[
            pl.BlockSpec((tm, d), row),
            pl.BlockSpec((1, d), full),
            pl.BlockSpec((d, 2 * d), full),
            pl.BlockSpec((1, 2 * d), full),
            pl.BlockSpec((tm, A_HALF), row),
            pl.BlockSpec((tm, ATT_W), row),
            pl.BlockSpec((A_HALF, d), full),
            pl.BlockSpec((ATT_W, d), full),
            pl.BlockSpec((d, d), full),
        ],
        out_specs=pl.BlockSpec((tm, d), row),
        out_shape=jax.ShapeDtypeStruct((n, d), F32),
        compiler_params=_cparams(1),
        name="merge",
    )(x, g.reshape(1, d), w_g, b_gate.reshape(1, 2 * d), ya, yb, w_pa, w_pb, w_out)


def _rope_tables(seq):
    inv = ROPE_THETA ** (-jnp.arange(0, HEAD_DIM, 2, dtype=F32) / HEAD_DIM)
    ang = jnp.arange(seq, dtype=jnp.int32).astype(F32)[:, None] * inv[None, :]
    cos, sin = jnp.cos(ang), jnp.sin(ang)
    reps = LANES // HEAD_DIM
    cos2 = jnp.tile(jnp.concatenate([cos, cos], axis=1), (1, reps))
    sin2 = jnp.tile(jnp.concatenate([-sin, sin], axis=1), (1, reps))
    return cos2, sin2


def kernel(x, ffn1_norm, ffn1_w_gu, ffn1_w_down, mix_norm, w_in, b_gate, gmlp_ln_g, gmlp_ln_b, gmlp_w_s, gmlp_b_s, w_pa, w_pb, w_out, ffn2_norm, ffn2_w_gu, ffn2_w_down, final_norm):
    b, t, d = x.shape
    depth = ffn1_norm.shape[0]
    n = b * t
    cos2, sin2 = _rope_tables(t)
    c_uv = 2 * A_HALF
    c_qkv = c_uv + 3 * ATT_W
    c_qi = c_qkv + IDX_HEADS * IDX_DIM
    c_kw = c_qi + IDX_DIM + IDX_HEADS
    xf = x.reshape(n, d)
    for l in range(depth):
        w_in_l = w_in[l].astype(BF16)
        w_kw = jnp.pad(w_in_l[:, c_qi:c_kw], ((0, 0), (0, LANES - (c_kw - c_qi))))
        last = l == depth - 1
        xf = _ffn(xf, ffn1_norm[l], ffn1_w_gu[l].astype(BF16), ffn1_w_down[l].astype(BF16))
        ya = _gmlp(xf, mix_norm[l], w_in_l[:, :c_uv], gmlp_ln_g[l], gmlp_ln_b[l], gmlp_w_s[l], gmlp_b_s[l])
        q, k, v, qi, ki, wi = _prep(xf, mix_norm[l], w_in_l[:, c_uv:c_qkv], w_in_l[:, c_qkv:c_qi], w_kw,
                                    cos2, sin2, t)
        r3 = lambda a: a.reshape(b, t, a.shape[-1])
        yb = _attention(r3(q), r3(qi), r3(wi), r3(k), r3(v), r3(ki)).reshape(n, ATT_W)
        xf = _merge(xf, mix_norm[l], w_in_l[:, c_kw:], b_gate[l], ya, yb,
                    w_pa[l].astype(BF16), w_pb[l].astype(BF16), w_out[l].astype(BF16))
        xf = _ffn(xf, ffn2_norm[l], ffn2_w_gu[l].astype(BF16), ffn2_w_down[l].astype(BF16),
                  final_g=final_norm if last else None)
    return xf.reshape(b, t, d)
```

```python
import functools

import jax
import jax.numpy as jnp
import numpy as np
from jax import lax
from jax.experimental import pallas as pl
from jax.experimental.pallas import tpu as pltpu

F32 = jnp.float32
BF16 = jnp.bfloat16

CHUNK = 128
A_GROUPS = 4
A_GROUP_CH = 128
A_HALF = A_GROUPS * A_GROUP_CH
N_HEADS = 8
HEAD_DIM = 64
ATT_W = N_HEADS * HEAD_DIM
IDX_HEADS = 8
IDX_DIM = 64
IDX_W = IDX_HEADS * IDX_DIM
TOPK_MAX = 256
ROPE_THETA = 10000.0
EPS = 1e-6
IDX_SCALE = (IDX_DIM ** -0.5) * (IDX_HEADS ** -0.5)
ATT_SCALE = HEAD_DIM ** -0.5

LANES = 128
SUBLANES = 8
VMEM_LIMIT_BYTES = 56 * 1024 * 1024

FFN_TM = 512
FFN_TF_TILES = 2
ROW_TM = 512
ATT_BQ = 256
ATT_BK = ROW_TM
RED_ROWS = 64

KEY_NEG_F32_MAX = np.int32(-2139095040)
F32_MAX = float(np.finfo(np.float32).max)
NT_DIMS = (((1,), (1,)), ((), ()))


def _rms(x, g):
    return x * lax.rsqrt(jnp.mean(x * x, axis=-1, keepdims=True) + EPS) * g


def _cparams(n_axes):
    return pltpu.CompilerParams(
        dimension_semantics=("arbitrary",) * n_axes, vmem_limit_bytes=VMEM_LIMIT_BYTES)


def _ffn_kernel(x_ref, g_ref, wg_ref, wu_ref, wd_ref, *rest, n_tiles, final):
    if final:
        gf_ref, o_ref, hn_ref, acc_ref = rest
    else:
        o_ref, hn_ref, acc_ref = rest
    j = pl.program_id(1)

    @pl.when(j == 0)
    def _():
        hn_ref[...] = _rms(x_ref[...], g_ref[...]).astype(BF16)
        acc_ref[...] = jnp.zeros_like(acc_ref)

    hn = hn_ref[...]
    gate = jnp.dot(hn, wg_ref[...], preferred_element_type=F32)
    up = jnp.dot(hn, wu_ref[...], preferred_element_type=F32)
    a = (jax.nn.silu(gate) * up).astype(BF16)
    acc_ref[...] += jnp.dot(a, wd_ref[...], preferred_element_type=F32)

    @pl.when(j == n_tiles - 1)
    def _():
        y = x_ref[...] + 0.5 * acc_ref[...]
        if final:
            y = _rms(y, gf_ref[...])
        o_ref[...] = y


def _ffn(x, g, w_gu, w_down, final_g=None):
    n, d = x.shape
    f = w_down.shape[0]
    tm, nt = FFN_TM, FFN_TF_TILES
    tf = f // nt
    assert n % tm == 0 and f % nt == 0 and tf % LANES == 0
    final = final_g is not None
    in_specs = [
        pl.BlockSpec((tm, d), lambda i, j: (i, 0)),
        pl.BlockSpec((1, d), lambda i, j: (0, 0)),
        pl.BlockSpec((d, tf), lambda i, j: (0, j)),
        pl.BlockSpec((d, tf), lambda i, j: (0, j + nt)),
        pl.BlockSpec((tf, d), lambda i, j: (j, 0)),
    ]
    args = [x, g.reshape(1, d), w_gu, w_gu, w_down]
    if final:
        in_specs.append(pl.BlockSpec((1, d), lambda i, j: (0, 0)))
        args.append(final_g.reshape(1, d))
    return pl.pallas_call(
        functools.partial(_ffn_kernel, n_tiles=nt, final=final),
        grid=(n // tm, nt),
        in_specs=in_specs,
        out_specs=pl.BlockSpec((tm, d), lambda i, j: (i, 0)),
        out_shape=jax.ShapeDtypeStruct((n, d), F32),
        scratch_shapes=[pltpu.VMEM((tm, d), BF16), pltpu.VMEM((tm, d), F32)],
        compiler_params=_cparams(2),
        name="ffn",
    )(*args)


def _gmlp_kernel(x_ref, g_ref, wuv_ref, lng_ref, lnb_ref, ws_ref, bs_ref, o_ref, *, tm):
    h = _rms(x_ref[...], g_ref[...]).astype(BF16)
    uv = jnp.dot(h, wuv_ref[...], preferred_element_type=F32)
    u = jax.nn.gelu(uv[:, :A_HALF])
    v = jax.nn.gelu(uv[:, A_HALF:])
    mu = jnp.mean(v, axis=-1, keepdims=True)
    var = jnp.mean(jnp.square(v - mu), axis=-1, keepdims=True)
    v = ((v - mu) * lax.rsqrt(var + EPS) * lng_ref[...] + lnb_ref[...]).astype(BF16)
    tri = (lax.broadcasted_iota(jnp.int32, (CHUNK, CHUNK), 1)
           <= lax.broadcasted_iota(jnp.int32, (CHUNK, CHUNK), 0))
    for g in range(A_GROUPS):
        ws = jnp.where(tri, ws_ref[g], 0.0).astype(BF16)
        cols = slice(g * A_GROUP_CH, (g + 1) * A_GROUP_CH)
        for c in range(tm // CHUNK):
            rows = slice(c * CHUNK, (c + 1) * CHUNK)
            mixed = jnp.dot(ws, v[rows, cols], preferred_element_type=F32) + bs_ref[g]
            o_ref[rows, cols] = (u[rows, cols] * mixed).astype(BF16)


def _gmlp(x, g, w_uv, ln_g, ln_b, w_s, b_s):
    n, d = x.shape
    tm = ROW_TM
    assert n % tm == 0 and tm % CHUNK == 0
    bs_b = jnp.broadcast_to(b_s[:, :, None], (A_GROUPS, CHUNK, A_GROUP_CH))
    return pl.pallas_call(
        functools.partial(_gmlp_kernel, tm=tm),
        grid=(n // tm,),
        in_specs=[
            pl.BlockSpec((tm, d), lambda i: (i, 0)),
            pl.BlockSpec((1, d), lambda i: (0, 0)),
            pl.BlockSpec((d, 2 * A_HALF), lambda i: (0, 0)),
            pl.BlockSpec((1, A_HALF), lambda i: (0, 0)),
            pl.BlockSpec((1, A_HALF), lambda i: (0, 0)),
            pl.BlockSpec((A_GROUPS, CHUNK, CHUNK), lambda i: (0, 0, 0)),
            pl.BlockSpec((A_GROUPS, CHUNK, A_GROUP_CH), lambda i: (0, 0, 0)),
        ],
        out_specs=pl.BlockSpec((tm, A_HALF), lambda i: (i, 0)),
        out_shape=jax.ShapeDtypeStruct((n, A_HALF), BF16),
        compiler_params=_cparams(1),
        name="gmlp",
    )(x, g.reshape(1, d), w_uv, ln_g.reshape(1, A_HALF), ln_b.reshape(1, A_HALF), w_s, bs_b)


def _rope(x, cos2, sin2):
    lane = lax.broadcasted_iota(jnp.int32, (x.shape[0], LANES), 1)
    first_half = (lane % HEAD_DIM) < (HEAD_DIM // 2)
    outs = []
    for c in range(x.shape[1] // LANES):
        xb = x[:, c * LANES:(c + 1) * LANES]
        partner = jnp.where(first_half,
                            pltpu.roll(xb, LANES - HEAD_DIM // 2, axis=1),
                            pltpu.roll(xb, HEAD_DIM // 2, axis=1))
        outs.append(xb * cos2 + partner * sin2)
    return outs[0] if len(outs) == 1 else jnp.concatenate(outs, axis=1)


def _prep_kernel(x_ref, g_ref, wqkv_ref, wqi_ref, wki_ref, wwi_ref, cos_ref, sin_ref,
                 q_ref, k_ref, vt_ref, qi_ref, ki_ref, wit_ref):
    h = _rms(x_ref[...], g_ref[...]).astype(BF16)
    cos2, sin2 = cos_ref[...], sin_ref[...]
    qkv = jnp.dot(h, wqkv_ref[...], preferred_element_type=F32)
    q_ref[...] = (_rope(qkv[:, :ATT_W], cos2, sin2) * ATT_SCALE).astype(BF16)
    k_ref[...] = _rope(qkv[:, ATT_W:2 * ATT_W], cos2, sin2).astype(BF16)
    vt_ref[0, 0] = qkv[:, 2 * ATT_W:].T.astype(BF16)
    qi = jnp.dot(h, wqi_ref[...], preferred_element_type=F32)
    qi_ref[...] = _rope(qi, cos2, sin2).astype(BF16)
    ki2 = jnp.dot(h, wki_ref[...], preferred_element_type=F32)
    ki_ref[...] = _rope(ki2, cos2, sin2).astype(BF16)
    wi = jnp.dot(h, wwi_ref[...], preferred_element_type=F32)
    wit_ref[0] = wi.T[:IDX_HEADS, :]


def _prep(x, g, w_qkv, w_qi, w_ki2, w_wi, cos2, sin2, batch, seq):
    n, d = x.shape
    tm = ROW_TM
    assert n % tm == 0 and seq % tm == 0 and tm == ATT_BK
    tpb = seq // tm
    full = lambda i: (0, 0)
    row = lambda i: (i, 0)
    pos = lambda i: (i % tpb, 0)
    return pl.pallas_call(
        _prep_kernel,
        grid=(n // tm,),
        in_specs=[
            pl.BlockSpec((tm, d), row),
            pl.BlockSpec((1, d), full),
            pl.BlockSpec((d, 3 * ATT_W), full),
            pl.BlockSpec((d, IDX_W), full),
            pl.BlockSpec((d, LANES), full),
            pl.BlockSpec((d, LANES), full),
            pl.BlockSpec((tm, LANES), pos),
            pl.BlockSpec((tm, LANES), pos),
        ],
        out_specs=[
            pl.BlockSpec((tm, ATT_W), row),
            pl.BlockSpec((tm, ATT_W), row),
            pl.BlockSpec((1, 1, ATT_W, tm), lambda i: (i // tpb, i % tpb, 0, 0)),
            pl.BlockSpec((tm, IDX_W), row),
            pl.BlockSpec((tm, LANES), row),
            pl.BlockSpec((1, IDX_HEADS, tm), lambda i: (i // tpb, 0, i % tpb)),
        ],
        out_shape=[
            jax.ShapeDtypeStruct((n, ATT_W), BF16),
            jax.ShapeDtypeStruct((n, ATT_W), BF16),
            jax.ShapeDtypeStruct((batch, tpb, ATT_W, tm), BF16),
            jax.ShapeDtypeStruct((n, IDX_W), BF16),
            jax.ShapeDtypeStruct((n, LANES), BF16),
            jax.ShapeDtypeStruct((batch, IDX_HEADS, seq), F32),
        ],
        compiler_params=_cparams(1),
        name="attn_prep",
    )(x, g.reshape(1, d), w_qkv, w_qi, w_ki2, w_wi, cos2, sin2)


def _attn_kernel(q_ref, qi_ref, wit_ref, k_ref, vt_ref, ki_ref, o_ref,
                 sc_ref, hi_ref, lo_ref, qp_ref, qip_ref, s_ref, p_ref, cm_ref, m_ref, l_ref, acc_ref,
                 *, bq, bk, topk, idx_bits):
    i = pl.program_id(1)
    nk = (i * bq + bq + bk - 1) // bk
    topk_f = float(topk)
    pair = LANES // HEAD_DIM
    n_rg = bk // RED_ROWS

    def rg(r):
        return slice(r * RED_ROWS, (r + 1) * RED_ROWS)

    def chunk(kc):
        return pl.ds(pl.multiple_of(kc * bk, bk), bk)

    lane_head = lax.broadcasted_iota(jnp.int32, (bq, LANES), 1) // HEAD_DIM
    for h in range(N_HEADS):
        slab = slice((h // pair) * LANES, (h // pair + 1) * LANES)
        keep = lane_head == (h % pair)
        qp_ref[h] = jnp.where(keep, q_ref[0, :, slab], jnp.zeros((), BF16))
        qip_ref[h] = jnp.where(keep, qi_ref[0, :, slab], jnp.zeros((), BF16))

    qpos = i * bq + lax.broadcasted_iota(jnp.int32, (RED_ROWS, bq), 1)
    krow = lax.broadcasted_iota(jnp.int32, (RED_ROWS, bq), 0)
    wit = wit_ref[0]

    def idx_body(kc, carry):
        kib = ki_ref[0, chunk(kc), :]
        acc = jnp.zeros((bk, bq), F32)
        for h in range(IDX_HEADS):
            l = lax.dot_general(kib, qip_ref[h], NT_DIMS, preferred_element_type=F32)
            acc = acc + jnp.maximum(l, 0.0) * wit[h:h + 1, :]
        for r in range(n_rg):
            kpos = kc * bk + r * RED_ROWS + krow
            sc = jnp.where(kpos <= qpos, acc[rg(r)] * IDX_SCALE, -jnp.inf)
            sc_ref[kc, rg(r), :] = sc
            bits = lax.bitcast_convert_type(sc, jnp.int32)
            key = jnp.where(bits < 0, bits ^ np.int32(0x7FFFFFFF), bits)
            hi_ref[kc, rg(r), :] = (key >> 16).astype(jnp.int16)
            lo_ref[kc, rg(r), :] = ((key & 0xFFFF) - 32768).astype(jnp.int16)
        return carry

    lax.fori_loop(0, nk, idx_body, 0)

    def reduce_keys(ref, fn, init, combine):
        def body(kc, acc):
            for r in range(n_rg):
                acc = combine(acc, fn(ref[kc, rg(r), :], kc * bk + r * RED_ROWS))
            return acc
        return lax.fori_loop(0, nk, body, init)

    one16, zero16, min16 = (jnp.full((), v, jnp.int16) for v in (1, 0, -32768))

    def count16(ref, pred):
        acc = reduce_keys(ref, lambda x, off: jnp.where(pred(x), one16, zero16),
                          jnp.zeros((RED_ROWS, bq), jnp.int16), jnp.add)
        return jnp.sum(acc.astype(jnp.int32), axis=0, keepdims=True)

    def count(pred):
        acc = reduce_keys(sc_ref, lambda s, off: jnp.where(pred(s, off), 1.0, 0.0),
                          jnp.zeros((RED_ROWS, bq), F32), jnp.add)
        return jnp.sum(acc, axis=0, keepdims=True)

    def key_min(fn):
        acc = reduce_keys(sc_ref, lambda s, off: fn(s), jnp.full((RED_ROWS, bq), jnp.inf, F32), jnp.minimum)
        return jnp.min(acc, axis=0, keepdims=True)

    def bisect16(ref, base):
        def body(it, u):
            trial = u | jnp.left_shift(jnp.int32(1), 15 - it)
            cand = (trial - 32768).astype(jnp.int16)
            cnt = base + count16(ref, lambda x: x >= cand)
            return jnp.where(cnt >= topk, trial, u)
        return lax.fori_loop(0, 16, body, jnp.zeros((1, bq), jnp.int32))

    hi_star = bisect16(hi_ref, 0) - 32768
    hi16 = hi_star.astype(jnp.int16)

    def narrow_body(kc, acc):
        for r in range(n_rg):
            hi = hi_ref[kc, rg(r), :]
            lo_ref[kc, rg(r), :] = jnp.where(hi == hi16, lo_ref[kc, rg(r), :], min16)
            acc = acc + jnp.where(hi > hi16, one16, zero16)
        return acc

    above = lax.fori_loop(0, nk, narrow_body, jnp.zeros((RED_ROWS, bq), jnp.int16))
    above = jnp.sum(above.astype(jnp.int32), axis=0, keepdims=True)
    lo_star = bisect16(lo_ref, above)
    key = jnp.maximum(hi_star * 65536 + lo_star, KEY_NEG_F32_MAX)
    thr0 = lax.bitcast_convert_type(jnp.where(key < 0, key ^ np.int32(0x7FFFFFFF), key), F32)

    t = key_min(lambda s: jnp.where(s >= thr0, s, jnp.inf))
    c_gt = count(lambda s, off: s > t)

    def adv_body(st):
        t, c_gt, _ = st
        t_next = key_min(lambda s: jnp.where(s > t, s, jnp.inf))
        t = jnp.where(c_gt >= topk_f, t_next, t)
        c_gt = count(lambda s, off: s > t)
        return t, c_gt, jnp.max(c_gt)

    t, c_gt, _ = lax.while_loop(lambda st: st[2] >= topk_f, adv_body, (t, c_gt, jnp.max(c_gt)))

    c_ge = count(lambda s, off: s >= t)
    rem = topk_f - c_gt

    def tie_split():
        def body(it, p):
            trial = p | jnp.left_shift(jnp.int32(1), idx_bits - 1 - it)
            cnt = count(lambda s, off: (s == t) & (off + krow < trial))
            return jnp.where(cnt < rem, trial, p)
        return lax.fori_loop(0, idx_bits, body, jnp.zeros((1, bq), jnp.int32))

    need_split = jnp.max(c_ge - c_gt - rem) > 0.0
    last_eq = lax.cond(need_split, tie_split, lambda: jnp.full((1, bq), 2 ** idx_bits, jnp.int32))

    def bias_body(kc, carry):
        for r in range(n_rg):
            s = sc_ref[kc, rg(r), :]
            sel = (s > t) | ((s == t) & (kc * bk + r * RED_ROWS + krow <= last_eq))
            sc_ref[kc, rg(r), :] = jnp.where(sel, 0.0, -jnp.inf)
        return carry

    lax.fori_loop(0, nk, bias_body, 0)

    m_ref[...] = jnp.full(m_ref.shape, -jnp.inf, F32)
    l_ref[...] = jnp.zeros(l_ref.shape, F32)
    acc_ref[...] = jnp.zeros(acc_ref.shape, F32)

    def logits(kc):
        slot = kc % 2
        for h in range(N_HEADS):
            slab = slice((h // pair) * LANES, (h // pair + 1) * LANES)
            s = lax.dot_general(k_ref[0, chunk(kc), slab], qp_ref[h], NT_DIMS,
                                preferred_element_type=F32)
            mx = jnp.full((RED_ROWS, bq), -jnp.inf, F32)
            for r in range(n_rg):
                sb = s[rg(r)] + sc_ref[kc, rg(r), :]
                s_ref[slot, h, rg(r), :] = sb
                mx = jnp.maximum(mx, sb)
            cm_ref[slot, h:h + 1, :] = jnp.max(mx, axis=0, keepdims=True)

    def accumulate(kc):
        slot = kc % 2
        for h in range(N_HEADS):
            hrows = slice(h * HEAD_DIM, (h + 1) * HEAD_DIM)
            m_old = m_ref[h:h + 1, :]
            m_new = jnp.maximum(m_old, cm_ref[slot, h:h + 1, :])
            m_use = jnp.where(m_new == -jnp.inf, 0.0, m_new)
            alpha = jnp.exp(m_old - m_use)
            psum = jnp.zeros((RED_ROWS, bq), F32)
            for r in range(n_rg):
                p = jnp.exp(s_ref[slot, h, rg(r), :] - m_use)
                psum = psum + p
                p_ref[h, rg(r), :] = p.astype(BF16)
            l_ref[h:h + 1, :] = alpha * l_ref[h:h + 1, :] + jnp.sum(psum, axis=0, keepdims=True)
            pv = jnp.dot(vt_ref[0, kc, hrows, :], p_ref[h], preferred_element_type=F32)
            acc_ref[hrows, :] = alpha * acc_ref[hrows, :] + pv
            m_ref[h:h + 1, :] = m_new

    logits(jnp.int32(0))

    def att_body(kc, carry):
        logits(kc + 1)
        accumulate(kc)
        return carry

    lax.fori_loop(0, nk - 1, att_body, 0)
    accumulate(nk - 1)

    outs = [acc_ref[h * HEAD_DIM:(h + 1) * HEAD_DIM, :] / l_ref[h:h + 1, :] for h in range(N_HEADS)]
    o_ref[0] = jnp.concatenate(outs, axis=0).T.astype(BF16)


def _attention(q, qi, wit, k, vt, ki):
    b, t, _ = q.shape
    bq, bk = ATT_BQ, ATT_BK
    topk = min(TOPK_MAX, t // 4)
    assert t % bk == 0 and t % bq == 0 and bq % LANES == 0 and bk % RED_ROWS == 0 and bk >= topk
    assert (bk // RED_ROWS) * (t // bk) < 2 ** 15
    idx_bits = max(1, int(np.ceil(np.log2(t))))
    qspec = lambda w: pl.BlockSpec((1, bq, w), lambda bi, i: (bi, i, 0))
    kspec = lambda w: pl.BlockSpec((1, t, w), lambda bi, i: (bi, 0, 0))
    return pl.pallas_call(
        functools.partial(_attn_kernel, bq=bq, bk=bk, topk=topk, idx_bits=idx_bits),
        grid=(b, t // bq),
        in_specs=[
            qspec(ATT_W), qspec(IDX_W),
            pl.BlockSpec((1, IDX_HEADS, bq), lambda bi, i: (bi, 0, i)),
            kspec(ATT_W),
            pl.BlockSpec((1, t // bk, ATT_W, bk), lambda bi, i: (bi, 0, 0, 0)),
            kspec(LANES),
        ],
        out_specs=qspec(ATT_W),
        out_shape=jax.ShapeDtypeStruct((b, t, ATT_W), BF16),
        scratch_shapes=[
            pltpu.VMEM((t // bk, bk, bq), F32),
            pltpu.VMEM((t // bk, bk, bq), jnp.int16),
            pltpu.VMEM((t // bk, bk, bq), jnp.int16),
            pltpu.VMEM((N_HEADS, bq, LANES), BF16),
            pltpu.VMEM((IDX_HEADS, bq, LANES), BF16),
            pltpu.VMEM((2, N_HEADS, bk, bq), F32),
            pltpu.VMEM((N_HEADS, bk, bq), BF16),
            pltpu.VMEM((2, N_HEADS, bq), F32),
            pltpu.VMEM((N_HEADS, bq), F32),
            pltpu.VMEM((N_HEADS, bq), F32),
            pltpu.VMEM((ATT_W, bq), F32),
        ],
        compiler_params=_cparams(2),
        name="dsa_attention",
    )(q, qi, wit, k, vt, ki)


def _merge_kernel(x_ref, g_ref, wg_ref, bg_ref, ya_ref, yb_ref, wpa_ref, wpb_ref, wo_ref, o_ref):
    x = x_ref[...]
    d = x.shape[1]
    h = _rms(x, g_ref[...]).astype(BF16)
    gates = jax.nn.sigmoid(jnp.dot(h, wg_ref[...], preferred_element_type=F32) + bg_ref[...])
    ya = jnp.dot(ya_ref[...], wpa_ref[...], preferred_element_type=F32)
    yb = jnp.dot(yb_ref[...], wpb_ref[...], preferred_element_type=F32)
    m = (gates[:, :d] * ya + gates[:, d:] * yb).astype(BF16)
    o_ref[...] = x + jnp.dot(m, wo_ref[...], preferred_element_type=F32)


def _merge(x, g, w_g, b_gate, ya, yb, w_pa, w_pb, w_out):
    n, d = x.shape
    tm = ROW_TM
    assert n % tm == 0
    full = lambda i: (0, 0)
    row = lambda i: (i, 0)
    return pl.pallas_call(
        _merge_kernel,
        grid=(n // tm,),
        in_specs=[
            pl.BlockSpec((tm, d), row),
            pl.BlockSpec((1, d), full),
            pl.BlockSpec((d, 2 * d), full),
            pl.BlockSpec((1, 2 * d), full),
            pl.BlockSpec((tm, A_HALF), row),
            pl.BlockSpec((tm, ATT_W), row),
            pl.BlockSpec((A_HALF, d), full),
            pl.BlockSpec((ATT_W, d), full),
            pl.BlockSpec((d, d), full),
        ],
        out_specs=pl.BlockSpec((tm, d), row),
        out_shape=jax.ShapeDtypeStruct((n, d), F32),
        compiler_params=_cparams(1),
        name="merge",
    )(x, g.reshape(1, d), w_g, b_gate.reshape(1, 2 * d), ya, yb, w_pa, w_pb, w_out)


def _rope_tables(seq):
    inv = ROPE_THETA ** (-jnp.arange(0, HEAD_DIM, 2, dtype=F32) / HEAD_DIM)
    ang = jnp.arange(seq, dtype=jnp.int32).astype(F32)[:, None] * inv[None, :]
    cos, sin = jnp.cos(ang), jnp.sin(ang)
    reps = LANES // HEAD_DIM
    cos2 = jnp.tile(jnp.concatenate([cos, cos], axis=1), (1, reps))
    sin2 = jnp.tile(jnp.concatenate([-sin, sin], axis=1), (1, reps))
    return cos2, sin2


def kernel(x, ffn1_norm, ffn1_w_gu, ffn1_w_down, mix_norm, w_in, b_gate, gmlp_ln_g, gmlp_ln_b, gmlp_w_s, gmlp_b_s, w_pa, w_pb, w_out, ffn2_norm, ffn2_w_gu, ffn2_w_down, final_norm):
    b, t, d = x.shape
    depth = ffn1_norm.shape[0]
    n = b * t
    cos2, sin2 = _rope_tables(t)
    c_uv = 2 * A_HALF
    c_qkv = c_uv + 3 * ATT_W
    c_qi = c_qkv + IDX_W
    c_ki = c_qi + IDX_DIM
    c_wi = c_ki + IDX_HEADS
    xf = x.reshape(n, d)
    for l in range(depth):
        w_in_l = w_in[l].astype(BF16)
        w_ki2 = jnp.concatenate([w_in_l[:, c_qi:c_ki]] * (LANES // IDX_DIM), axis=1)
        w_wi = jnp.pad(w_in_l[:, c_ki:c_wi], ((0, 0), (0, LANES - IDX_HEADS)))
        last = l == depth - 1
        xf = _ffn(xf, ffn1_norm[l], ffn1_w_gu[l].astype(BF16), ffn1_w_down[l].astype(BF16))
        ya = _gmlp(xf, mix_norm[l], w_in_l[:, :c_uv], gmlp_ln_g[l], gmlp_ln_b[l], gmlp_w_s[l], gmlp_b_s[l])
        q, k, vt, qi, ki, wit = _prep(xf, mix_norm[l], w_in_l[:, c_uv:c_qkv], w_in_l[:, c_qkv:c_qi],
                                      w_ki2, w_wi, cos2, sin2, b, t)
        r3 = lambda a: a.reshape(b, t, a.shape[-1])
        yb = _attention(r3(q), r3(qi), wit, r3(k), vt, r3(ki)).reshape(n, ATT_W)
        xf = _merge(xf, mix_norm[l], w_in_l[:, c_wi:], b_gate[l], ya, yb,
                    w_pa[l].astype(BF16), w_pb[l].astype(BF16), w_out[l].astype(BF16))
        xf = _ffn(xf, ffn2_norm[l], ffn2_w_gu[l].astype(BF16), ffn2_w_down[l].astype(BF16),
                  final_g=final_norm if last else None)
    return xf.reshape(b, t, d)
```

```python
import functools

import jax
import jax.numpy as jnp
import numpy as np
from jax import lax
from jax.experimental import pallas as pl
from jax.experimental.pallas import tpu as pltpu

F32 = jnp.float32
BF16 = jnp.bfloat16

CHUNK = 128
A_GROUPS = 4
A_GROUP_CH = 128
A_HALF = A_GROUPS * A_GROUP_CH
N_HEADS = 8
HEAD_DIM = 64
ATT_W = N_HEADS * HEAD_DIM
IDX_HEADS = 8
IDX_DIM = 64
IDX_W = IDX_HEADS * IDX_DIM
TOPK_MAX = 256
ROPE_THETA = 10000.0
EPS = 1e-6
IDX_SCALE = (IDX_DIM ** -0.5) * (IDX_HEADS ** -0.5)
ATT_SCALE = HEAD_DIM ** -0.5
LOG2_E = float(np.log2(np.e))

LANES = 128
SUBLANES = 8
VMEM_LIMIT_BYTES = 56 * 1024 * 1024

FFN_TM = 512
ROW_TM = 512
ATT_BQ = 256
ATT_BK = ROW_TM
RED_ROWS = 64
IDX_SPLIT = 4

KEY_NEG_F32_MAX = np.int32(-2139095040)
F32_MAX = float(np.finfo(np.float32).max)
NT_DIMS = (((1,), (1,)), ((), ()))


def _rms(x, g):
    return x * lax.rsqrt(jnp.mean(x * x, axis=-1, keepdims=True) + EPS) * g


def _cparams(n_axes):
    return pltpu.CompilerParams(
        dimension_semantics=("arbitrary",) * n_axes, vmem_limit_bytes=VMEM_LIMIT_BYTES)


def _ffn_kernel(x_ref, g_ref, wg_ref, wu_ref, wd_ref, *rest, final):
    if final:
        gf_ref, o_ref = rest
    else:
        (o_ref,) = rest
    x = x_ref[...]
    hn = _rms(x, g_ref[...]).astype(BF16)
    gate = jnp.dot(hn, wg_ref[...], preferred_element_type=F32)
    up = jnp.dot(hn, wu_ref[...], preferred_element_type=F32)
    a = (jax.nn.silu(gate) * up).astype(BF16)
    y = x + 0.5 * jnp.dot(a, wd_ref[...], preferred_element_type=F32)
    if final:
        y = _rms(y, gf_ref[...])
    o_ref[...] = y


def _ffn(x, g, w_gu, w_down, final_g=None):
    n, d = x.shape
    f = w_down.shape[0]
    tm = FFN_TM
    assert n % tm == 0 and f % LANES == 0
    final = final_g is not None
    resident = dict(pipeline_mode=pl.Buffered(1))
    in_specs = [
        pl.BlockSpec((tm, d), lambda i: (i, 0)),
        pl.BlockSpec((1, d), lambda i: (0, 0)),
        pl.BlockSpec((d, f), lambda i: (0, 0), **resident),
        pl.BlockSpec((d, f), lambda i: (0, 1), **resident),
        pl.BlockSpec((f, d), lambda i: (0, 0), **resident),
    ]
    args = [x, g.reshape(1, d), w_gu, w_gu, w_down]
    if final:
        in_specs.append(pl.BlockSpec((1, d), lambda i: (0, 0)))
        args.append(final_g.reshape(1, d))
    return pl.pallas_call(
        functools.partial(_ffn_kernel, final=final),
        grid=(n // tm,),
        in_specs=in_specs,
        out_specs=pl.BlockSpec((tm, d), lambda i: (i, 0)),
        out_shape=jax.ShapeDtypeStruct((n, d), F32),
        compiler_params=_cparams(1),
        name="ffn",
    )(*args)


def _gmlp_kernel(x_ref, g_ref, wuv_ref, lng_ref, lnb_ref, ws_ref, bs_ref, o_ref, *, tm):
    h = _rms(x_ref[...], g_ref[...]).astype(BF16)
    uv = jnp.dot(h, wuv_ref[...], preferred_element_type=F32)
    u = jax.nn.gelu(uv[:, :A_HALF])
    v = jax.nn.gelu(uv[:, A_HALF:])
    mu = jnp.mean(v, axis=-1, keepdims=True)
    var = jnp.mean(jnp.square(v - mu), axis=-1, keepdims=True)
    v = ((v - mu) * lax.rsqrt(var + EPS) * lng_ref[...] + lnb_ref[...]).astype(BF16)
    tri = (lax.broadcasted_iota(jnp.int32, (CHUNK, CHUNK), 1)
           <= lax.broadcasted_iota(jnp.int32, (CHUNK, CHUNK), 0))
    for g in range(A_GROUPS):
        ws = jnp.where(tri, ws_ref[g], 0.0).astype(BF16)
        cols = slice(g * A_GROUP_CH, (g + 1) * A_GROUP_CH)
        for c in range(tm // CHUNK):
            rows = slice(c * CHUNK, (c + 1) * CHUNK)
            mixed = jnp.dot(ws, v[rows, cols], preferred_element_type=F32) + bs_ref[g]
            o_ref[rows, cols] = (u[rows, cols] * mixed).astype(BF16)


def _gmlp(x, g, w_uv, ln_g, ln_b, w_s, b_s):
    n, d = x.shape
    tm = ROW_TM
    assert n % tm == 0 and tm % CHUNK == 0
    bs_b = jnp.broadcast_to(b_s[:, :, None], (A_GROUPS, CHUNK, A_GROUP_CH))
    return pl.pallas_call(
        functools.partial(_gmlp_kernel, tm=tm),
        grid=(n // tm,),
        in_specs=[
            pl.BlockSpec((tm, d), lambda i: (i, 0)),
            pl.BlockSpec((1, d), lambda i: (0, 0)),
            pl.BlockSpec((d, 2 * A_HALF), lambda i: (0, 0)),
            pl.BlockSpec((1, A_HALF), lambda i: (0, 0)),
            pl.BlockSpec((1, A_HALF), lambda i: (0, 0)),
            pl.BlockSpec((A_GROUPS, CHUNK, CHUNK), lambda i: (0, 0, 0)),
            pl.BlockSpec((A_GROUPS, CHUNK, A_GROUP_CH), lambda i: (0, 0, 0)),
        ],
        out_specs=pl.BlockSpec((tm, A_HALF), lambda i: (i, 0)),
        out_shape=jax.ShapeDtypeStruct((n, A_HALF), BF16),
        compiler_params=_cparams(1),
        name="gmlp",
    )(x, g.reshape(1, d), w_uv, ln_g.reshape(1, A_HALF), ln_b.reshape(1, A_HALF), w_s, bs_b)


def _rope(x, cos2, sin2):
    lane = lax.broadcasted_iota(jnp.int32, (x.shape[0], LANES), 1)
    first_half = (lane % HEAD_DIM) < (HEAD_DIM // 2)
    outs = []
    for c in range(x.shape[1] // LANES):
        xb = x[:, c * LANES:(c + 1) * LANES]
        partner = jnp.where(first_half,
                            pltpu.roll(xb, LANES - HEAD_DIM // 2, axis=1),
                            pltpu.roll(xb, HEAD_DIM // 2, axis=1))
        outs.append(xb * cos2 + partner * sin2)
    return outs[0] if len(outs) == 1 else jnp.concatenate(outs, axis=1)


def _prep_kernel(x_ref, g_ref, wqkv_ref, wqi_ref, wki_ref, wwi_ref, cos_ref, sin_ref,
                 q_ref, k_ref, vt_ref, qi_ref, ki_ref, wit_ref):
    h = _rms(x_ref[...], g_ref[...]).astype(BF16)
    cos2, sin2 = cos_ref[...], sin_ref[...]
    qkv = jnp.dot(h, wqkv_ref[...], preferred_element_type=F32)
    q_ref[...] = (_rope(qkv[:, :ATT_W], cos2, sin2) * (ATT_SCALE * LOG2_E)).astype(BF16)
    k = _rope(qkv[:, ATT_W:2 * ATT_W], cos2, sin2).astype(BF16)
    for c in range(ATT_W // LANES):
        k_ref[0, c] = k[:, c * LANES:(c + 1) * LANES]
    vt_ref[0, 0] = qkv[:, 2 * ATT_W:].T.astype(BF16)
    qi = jnp.dot(h, wqi_ref[...], preferred_element_type=F32)
    qi_ref[...] = _rope(qi, cos2, sin2).astype(BF16)
    ki2 = jnp.dot(h, wki_ref[...], preferred_element_type=F32)
    ki_ref[...] = _rope(ki2, cos2, sin2).astype(BF16)
    wi = jnp.dot(h, wwi_ref[...], preferred_element_type=F32)
    wit_ref[0] = wi.T[:IDX_HEADS, :]


def _prep(x, g, w_qkv, w_qi, w_ki2, w_wi, cos2, sin2, batch, seq):
    n, d = x.shape
    tm = ROW_TM
    assert n % tm == 0 and seq % tm == 0 and tm == ATT_BK
    tpb = seq // tm
    full = lambda i: (0, 0)
    row = lambda i: (i, 0)
    pos = lambda i: (i % tpb, 0)
    return pl.pallas_call(
        _prep_kernel,
        grid=(n // tm,),
        in_specs=[
            pl.BlockSpec((tm, d), row),
            pl.BlockSpec((1, d), full),
            pl.BlockSpec((d, 3 * ATT_W), full),
            pl.BlockSpec((d, IDX_W), full),
            pl.BlockSpec((d, LANES), full),
            pl.BlockSpec((d, LANES), full),
            pl.BlockSpec((tm, LANES), pos),
            pl.BlockSpec((tm, LANES), pos),
        ],
        out_specs=[
            pl.BlockSpec((tm, ATT_W), row),
            pl.BlockSpec((1, ATT_W // LANES, tm, LANES), lambda i: (i // tpb, 0, i % tpb, 0)),
            pl.BlockSpec((1, 1, ATT_W, tm), lambda i: (i // tpb, i % tpb, 0, 0)),
            pl.BlockSpec((tm, IDX_W), row),
            pl.BlockSpec((tm, LANES), row),
            pl.BlockSpec((1, IDX_HEADS, tm), lambda i: (i // tpb, 0, i % tpb)),
        ],
        out_shape=[
            jax.ShapeDtypeStruct((n, ATT_W), BF16),
            jax.ShapeDtypeStruct((batch, ATT_W // LANES, seq, LANES), BF16),
            jax.ShapeDtypeStruct((batch, tpb, ATT_W, tm), BF16),
            jax.ShapeDtypeStruct((n, IDX_W), BF16),
            jax.ShapeDtypeStruct((n, LANES), BF16),
            jax.ShapeDtypeStruct((batch, IDX_HEADS, seq), F32),
        ],
        compiler_params=_cparams(1),
        name="attn_prep",
    )(x, g.reshape(1, d), w_qkv, w_qi, w_ki2, w_wi, cos2, sin2)


def _attn_kernel(q_ref, qi_ref, wit_ref, k_ref, vt_ref, ki_ref, o_ref,
                 sc_ref, hi_ref, lo_ref, qp_ref, qip_ref, s_ref, p_ref, cm_ref, m_ref, l_ref, acc_ref,
                 *, bq, bk, topk, idx_bits):
    i = pl.program_id(1)
    nk = (i * bq + bq + bk - 1) // bk
    topk_f = float(topk)
    pair = LANES // HEAD_DIM
    n_rg = bk // RED_ROWS

    def rg(r):
        return slice(r * RED_ROWS, (r + 1) * RED_ROWS)

    def chunk(kc):
        return pl.ds(pl.multiple_of(kc * bk, bk), bk)

    lane_head = lax.broadcasted_iota(jnp.int32, (bq, LANES), 1) // HEAD_DIM
    for h in range(N_HEADS):
        slab = slice((h // pair) * LANES, (h // pair + 1) * LANES)
        keep = lane_head == (h % pair)
        qp_ref[h] = jnp.where(keep, q_ref[0, :, slab], jnp.zeros((), BF16))
        qip_ref[h] = jnp.where(keep, qi_ref[0, :, slab], jnp.zeros((), BF16))

    qpos = i * bq + lax.broadcasted_iota(jnp.int32, (RED_ROWS, bq), 1)
    krow = lax.broadcasted_iota(jnp.int32, (RED_ROWS, bq), 0)
    wit = wit_ref[0]

    def idx_body(kc, carry):
        for half in range(IDX_SPLIT):
            hrows = bk // IDX_SPLIT
            base = half * hrows
            kib = ki_ref[0, pl.ds(pl.multiple_of(kc * bk + base, hrows), hrows), :]
            acc = jnp.zeros((hrows, bq), F32)
            for h in range(IDX_HEADS):
                l = lax.dot_general(kib, qip_ref[h], NT_DIMS, preferred_element_type=F32)
                acc = acc + jnp.maximum(l, 0.0) * wit[h:h + 1, :]
            for r in range(hrows // RED_ROWS):
                rows = slice(base + r * RED_ROWS, base + (r + 1) * RED_ROWS)
                kpos = kc * bk + base + r * RED_ROWS + krow
                sc = jnp.where(kpos <= qpos, acc[rg(r)] * IDX_SCALE, -jnp.inf)
                sc_ref[kc, rows, :] = sc
                bits = lax.bitcast_convert_type(sc, jnp.int32)
                key = jnp.where(bits < 0, bits ^ np.int32(0x7FFFFFFF), bits)
                hi_ref[kc, rows, :] = (key >> 16).astype(jnp.int16)
                lo_ref[kc, rows, :] = ((key & 0xFFFF) - 32768).astype(jnp.int16)
        return carry

    lax.fori_loop(0, nk, idx_body, 0)

    def reduce_keys(ref, fn, init, combine):
        def body(kc, acc):
            for r in range(n_rg):
                acc = combine(acc, fn(ref[kc, rg(r), :], kc * bk + r * RED_ROWS))
            return acc
        return lax.fori_loop(0, nk, body, init)

    one16, zero16, min16 = (jnp.full((), v, jnp.int16) for v in (1, 0, -32768))

    def count16(ref, pred):
        acc = reduce_keys(ref, lambda x, off: jnp.where(pred(x), one16, zero16),
                          jnp.zeros((RED_ROWS, bq), jnp.int16), jnp.add)
        return jnp.sum(acc.astype(jnp.int32), axis=0, keepdims=True)

    def count(pred):
        acc = reduce_keys(sc_ref, lambda s, off: jnp.where(pred(s, off), 1.0, 0.0),
                          jnp.zeros((RED_ROWS, bq), F32), jnp.add)
        return jnp.sum(acc, axis=0, keepdims=True)

    def key_min(fn):
        acc = reduce_keys(sc_ref, lambda s, off: fn(s), jnp.full((RED_ROWS, bq), jnp.inf, F32), jnp.minimum)
        return jnp.min(acc, axis=0, keepdims=True)

    def bisect16(ref, base):
        def body(it, u):
            trial = u | jnp.left_shift(jnp.int32(1), 15 - it)
            cand = (trial - 32768).astype(jnp.int16)
            cnt = base + count16(ref, lambda x: x >= cand)
            return jnp.where(cnt >= topk, trial, u)
        return lax.fori_loop(0, 16, body, jnp.zeros((1, bq), jnp.int32))

    hi_star = bisect16(hi_ref, 0) - 32768
    hi16 = hi_star.astype(jnp.int16)

    def narrow_body(kc, acc):
        for r in range(n_rg):
            hi = hi_ref[kc, rg(r), :]
            lo_ref[kc, rg(r), :] = jnp.where(hi == hi16, lo_ref[kc, rg(r), :], min16)
            acc = acc + jnp.where(hi > hi16, one16, zero16)
        return acc

    above = lax.fori_loop(0, nk, narrow_body, jnp.zeros((RED_ROWS, bq), jnp.int16))
    above = jnp.sum(above.astype(jnp.int32), axis=0, keepdims=True)
    lo_star = bisect16(lo_ref, above)
    key = jnp.maximum(hi_star * 65536 + lo_star, KEY_NEG_F32_MAX)
    thr0 = lax.bitcast_convert_type(jnp.where(key < 0, key ^ np.int32(0x7FFFFFFF), key), F32)

    t = key_min(lambda s: jnp.where(s >= thr0, s, jnp.inf))
    c_gt = count(lambda s, off: s > t)

    def adv_body(st):
        t, c_gt, _ = st
        t_next = key_min(lambda s: jnp.where(s > t, s, jnp.inf))
        t = jnp.where(c_gt >= topk_f, t_next, t)
        c_gt = count(lambda s, off: s > t)
        return t, c_gt, jnp.max(c_gt)

    t, c_gt, _ = lax.while_loop(lambda st: st[2] >= topk_f, adv_body, (t, c_gt, jnp.max(c_gt)))

    c_ge = count(lambda s, off: s >= t)
    rem = topk_f - c_gt

    def tie_split():
        def body(it, p):
            trial = p | jnp.left_shift(jnp.int32(1), idx_bits - 1 - it)
            cnt = count(lambda s, off: (s == t) & (off + krow < trial))
            return jnp.where(cnt < rem, trial, p)
        return lax.fori_loop(0, idx_bits, body, jnp.zeros((1, bq), jnp.int32))

    need_split = jnp.max(c_ge - c_gt - rem) > 0.0
    last_eq = lax.cond(need_split, tie_split, lambda: jnp.full((1, bq), 2 ** idx_bits, jnp.int32))

    def bias_body(kc, carry):
        for r in range(n_rg):
            s = sc_ref[kc, rg(r), :]
            sel = (s > t) | ((s == t) & (kc * bk + r * RED_ROWS + krow <= last_eq))
            sc_ref[kc, rg(r), :] = jnp.where(sel, 0.0, -jnp.inf)
        return carry

    lax.fori_loop(0, nk, bias_body, 0)

    m_ref[...] = jnp.full(m_ref.shape, -jnp.inf, F32)
    l_ref[...] = jnp.zeros(l_ref.shape, F32)
    acc_ref[...] = jnp.zeros(acc_ref.shape, F32)

    def logits(kc, h):
        s = lax.dot_general(k_ref[0, h // pair, chunk(kc), :], qp_ref[h], NT_DIMS,
                            preferred_element_type=F32)
        mx = jnp.full((RED_ROWS, bq), -jnp.inf, F32)
        for r in range(n_rg):
            sb = s[rg(r)] + sc_ref[kc, rg(r), :]
            s_ref[h, rg(r), :] = sb
            mx = jnp.maximum(mx, sb)
        cm_ref[pl.ds(h, 1), :] = jnp.max(mx, axis=0, keepdims=True)

    def accumulate(kc, h):
        hrows = pl.ds(h * HEAD_DIM, HEAD_DIM)
        m_old = m_ref[pl.ds(h, 1), :]
        m_new = jnp.maximum(m_old, cm_ref[pl.ds(h, 1), :])
        m_use = jnp.where(m_new == -jnp.inf, 0.0, m_new)
        alpha = jnp.exp2(m_old - m_use)
        psum = jnp.zeros((RED_ROWS, bq), F32)
        for r in range(n_rg):
            p = jnp.exp2(s_ref[h, rg(r), :] - m_use)
            psum = psum + p
            p_ref[h, rg(r), :] = p.astype(BF16)
        l_ref[pl.ds(h, 1), :] = alpha * l_ref[pl.ds(h, 1), :] + jnp.sum(psum, axis=0, keepdims=True)
        pv = jnp.dot(vt_ref[0, kc, hrows, :], p_ref[h], preferred_element_type=F32)
        acc_ref[hrows, :] = alpha * acc_ref[hrows, :] + pv
        m_ref[pl.ds(h, 1), :] = m_new

    def for_heads(fn):
        for h in range(N_HEADS):
            fn(h)

    for_heads(lambda h: logits(jnp.int32(0), h))

    def att_body(kc, carry):
        def both(h):
            accumulate(kc, h)
            logits(kc + 1, h)
        for_heads(both)
        return carry

    lax.fori_loop(0, nk - 1, att_body, 0)
    for_heads(lambda h: accumulate(nk - 1, h))

    outs = [acc_ref[h * HEAD_DIM:(h + 1) * HEAD_DIM, :] / l_ref[h:h + 1, :] for h in range(N_HEADS)]
    o_ref[0] = jnp.concatenate(outs, axis=0).T.astype(BF16)


def _attention(q, qi, wit, k, vt, ki):
    b, t, _ = q.shape
    bq, bk = ATT_BQ, ATT_BK
    topk = min(TOPK_MAX, t // 4)
    assert t % bk == 0 and t % bq == 0 and bq % LANES == 0 and bk % RED_ROWS == 0 and bk >= topk
    assert (bk // RED_ROWS) * (t // bk) < 2 ** 15
    idx_bits = max(1, int(np.ceil(np.log2(t))))
    qspec = lambda w: pl.BlockSpec((1, bq, w), lambda bi, i: (bi, i, 0))
    kspec = lambda w: pl.BlockSpec((1, t, w), lambda bi, i: (bi, 0, 0))
    return pl.pallas_call(
        functools.partial(_attn_kernel, bq=bq, bk=bk, topk=topk, idx_bits=idx_bits),
        grid=(b, t // bq),
        in_specs=[
            qspec(ATT_W), qspec(IDX_W),
            pl.BlockSpec((1, IDX_HEADS, bq), lambda bi, i: (bi, 0, i)),
            pl.BlockSpec((1, ATT_W // LANES, t, LANES), lambda bi, i: (bi, 0, 0, 0)),
            pl.BlockSpec((1, t // bk, ATT_W, bk), lambda bi, i: (bi, 0, 0, 0)),
            kspec(LANES),
        ],
        out_specs=qspec(ATT_W),
        out_shape=jax.ShapeDtypeStruct((b, t, ATT_W), BF16),
        scratch_shapes=[
            pltpu.VMEM((t // bk, bk, bq), F32),
            pltpu.VMEM((t // bk, bk, bq), jnp.int16),
            pltpu.VMEM((t // bk, bk, bq), jnp.int16),
            pltpu.VMEM((N_HEADS, bq, LANES), BF16),
            pltpu.VMEM((IDX_HEADS, bq, LANES), BF16),
            pltpu.VMEM((N_HEADS, bk, bq), F32),
            pltpu.VMEM((N_HEADS, bk, bq), BF16),
            pltpu.VMEM((N_HEADS, bq), F32),
            pltpu.VMEM((N_HEADS, bq), F32),
            pltpu.VMEM((N_HEADS, bq), F32),
            pltpu.VMEM((ATT_W, bq), F32),
        ],
        compiler_params=_cparams(2),
        name="dsa_attention",
    )(q, qi, wit, k, vt, ki)


def _merge_kernel(x_ref, g_ref, wg_ref, bg_ref, ya_ref, yb_ref, wpa_ref, wpb_ref, wo_ref, o_ref):
    x = x_ref[...]
    d = x.shape[1]
    h = _rms(x, g_ref[...]).astype(BF16)
    gates = jax.nn.sigmoid(jnp.dot(h, wg_ref[...], preferred_element_type=F32) + bg_ref[...])
    ya = jnp.dot(ya_ref[...], wpa_ref[...], preferred_element_type=F32)
    yb = jnp.dot(yb_ref[...], wpb_ref[...], preferred_element_type=F32)
    m = (gates[:, :d] * ya + gates[:, d:] * yb).astype(BF16)
    o_ref[...] = x + jnp.dot(m, wo_ref[...], preferred_element_type=F32)


def _merge(x, g, w_g, b_gate, ya, yb, w_pa, w_pb, w_out):
    n, d = x.shape
    tm = ROW_TM
    assert n % tm == 0
    full = lambda i: (0, 0)
    row = lambda i: (i, 0)
    return pl.pallas_call(
        _merge_kernel,
        grid=(n // tm,),
        in_specs=[
            pl.BlockSpec((tm, d), row),
            pl.BlockSpec((1, d), full),
            pl.BlockSpec((d, 2 * d), full),
            pl.BlockSpec((1, 2 * d), full),
            pl.BlockSpec((tm, A_HALF), row),
            pl.BlockSpec((tm, ATT_W), row),
            pl.BlockSpec((A_HALF, d), full),
            pl.BlockSpec((ATT_W, d), full),
            pl.BlockSpec((d, d), full),
        ],
        out_specs=pl.BlockSpec((tm, d), row),
        out_shape=jax.ShapeDtypeStruct((n, d), F32),
        compiler_params=_cparams(1),
        name="merge",
    )(x, g.reshape(1, d), w_g, b_gate.reshape(1, 2 * d), ya, yb, w_pa, w_pb, w_out)


def _rope_tables(seq):
    inv = ROPE_THETA ** (-jnp.arange(0, HEAD_DIM, 2, dtype=F32) / HEAD_DIM)
    ang = jnp.arange(seq, dtype=jnp.int32).astype(F32)[:, None] * inv[None, :]
    cos, sin = jnp.cos(ang), jnp.sin(ang)
    reps = LANES // HEAD_DIM
    cos2 = jnp.tile(jnp.concatenate([cos, cos], axis=1), (1, reps))
    sin2 = jnp.tile(jnp.concatenate([-sin, sin], axis=1), (1, reps))
    return cos2, sin2


def kernel(x, ffn1_norm, ffn1_w_gu, ffn1_w_down, mix_norm, w_in, b_gate, gmlp_ln_g, gmlp_ln_b, gmlp_w_s, gmlp_b_s, w_pa, w_pb, w_out, ffn2_norm, ffn2_w_gu, ffn2_w_down, final_norm):
    b, t, d = x.shape
    depth = ffn1_norm.shape[0]
    n = b * t
    cos2, sin2 = _rope_tables(t)
    c_uv = 2 * A_HALF
    c_qkv = c_uv + 3 * ATT_W
    c_qi = c_qkv + IDX_W
    c_ki = c_qi + IDX_DIM
    c_wi = c_ki + IDX_HEADS
    xf = x.reshape(n, d)
    for l in range(depth):
        w_in_l = w_in[l].astype(BF16)
        w_ki2 = jnp.concatenate([w_in_l[:, c_qi:c_ki]] * (LANES // IDX_DIM), axis=1)
        w_wi = jnp.pad(w_in_l[:, c_ki:c_wi], ((0, 0), (0, LANES - IDX_HEADS)))
        last = l == depth - 1
        xf = _ffn(xf, ffn1_norm[l], ffn1_w_gu[l].astype(BF16), ffn1_w_down[l].astype(BF16))
        ya = _gmlp(xf, mix_norm[l], w_in_l[:, :c_uv], gmlp_ln_g[l], gmlp_ln_b[l], gmlp_w_s[l], gmlp_b_s[l])
        q, k, vt, qi, ki, wit = _prep(xf, mix_norm[l], w_in_l[:, c_uv:c_qkv], w_in_l[:, c_qkv:c_qi],
                                      w_ki2, w_wi, cos2, sin2, b, t)
        r3 = lambda a: a.reshape(b, t, a.shape[-1])
        yb = _attention(r3(q), r3(qi), wit, k, vt, r3(ki)).reshape(n, ATT_W)
        xf = _merge(xf, mix_norm[l], w_in_l[:, c_wi:], b_gate[l], ya, yb,
                    w_pa[l].astype(BF16), w_pb[l].astype(BF16), w_out[l].astype(BF16))
        xf = _ffn(xf, ffn2_norm[l], ffn2_w_gu[l].astype(BF16), ffn2_w_down[l].astype(BF16),
                  final_g=final_norm if last else None)
    return xf.reshape(b, t, d)
```

```python
import functools

import jax
import jax.numpy as jnp
import numpy as np
from jax import lax
from jax.experimental import pallas as pl
from jax.experimental.pallas import tpu as pltpu

F32 = jnp.float32
BF16 = jnp.bfloat16

CHUNK = 128
A_GROUPS = 4
A_GROUP_CH = 128
A_HALF = A_GROUPS * A_GROUP_CH
N_HEADS = 8
HEAD_DIM = 64
ATT_W = N_HEADS * HEAD_DIM
IDX_HEADS = 8
IDX_DIM = 64
IDX_W = IDX_HEADS * IDX_DIM
TOPK_MAX = 256
ROPE_THETA = 10000.0
EPS = 1e-6
IDX_SCALE = (IDX_DIM ** -0.5) * (IDX_HEADS ** -0.5)
ATT_SCALE = HEAD_DIM ** -0.5
LOG2_E = float(np.log2(np.e))

LANES = 128
SUBLANES = 8
VMEM_LIMIT_BYTES = 56 * 1024 * 1024

FFN_TM = 512
ROW_TM = 512
ATT_BQ = 256
ATT_BK = ROW_TM
RED_ROWS = 64
IDX_SPLIT = 4

KEY_NEG_F32_MAX = np.int32(-2139095040)
F32_MAX = float(np.finfo(np.float32).max)
NT_DIMS = (((1,), (1,)), ((), ()))


def _rms(x, g):
    return x * lax.rsqrt(jnp.mean(x * x, axis=-1, keepdims=True) + EPS) * g


def _cparams(n_axes):
    return pltpu.CompilerParams(
        dimension_semantics=("arbitrary",) * n_axes, vmem_limit_bytes=VMEM_LIMIT_BYTES)


def _ffn_kernel(x_ref, g_ref, wg_ref, wu_ref, wd_ref, *rest, final):
    if final:
        gf_ref, o_ref = rest
    else:
        (o_ref,) = rest
    x = x_ref[...]
    hn = _rms(x, g_ref[...]).astype(BF16)
    gate = jnp.dot(hn, wg_ref[...], preferred_element_type=F32)
    up = jnp.dot(hn, wu_ref[...], preferred_element_type=F32)
    a = (jax.nn.silu(gate) * up).astype(BF16)
    y = x + 0.5 * jnp.dot(a, wd_ref[...], preferred_element_type=F32)
    if final:
        y = _rms(y, gf_ref[...])
    o_ref[...] = y


def _ffn(x, g, w_gu, w_down, final_g=None):
    n, d = x.shape
    f = w_down.shape[0]
    tm = FFN_TM
    assert n % tm == 0 and f % LANES == 0
    final = final_g is not None
    resident = dict(pipeline_mode=pl.Buffered(1))
    in_specs = [
        pl.BlockSpec((tm, d), lambda i: (i, 0)),
        pl.BlockSpec((1, d), lambda i: (0, 0)),
        pl.BlockSpec((d, f), lambda i: (0, 0), **resident),
        pl.BlockSpec((d, f), lambda i: (0, 1), **resident),
        pl.BlockSpec((f, d), lambda i: (0, 0), **resident),
    ]
    args = [x, g.reshape(1, d), w_gu, w_gu, w_down]
    if final:
        in_specs.append(pl.BlockSpec((1, d), lambda i: (0, 0)))
        args.append(final_g.reshape(1, d))
    return pl.pallas_call(
        functools.partial(_ffn_kernel, final=final),
        grid=(n // tm,),
        in_specs=in_specs,
        out_specs=pl.BlockSpec((tm, d), lambda i: (i, 0)),
        out_shape=jax.ShapeDtypeStruct((n, d), F32),
        compiler_params=_cparams(1),
        name="ffn",
    )(*args)


def _gmlp_kernel(x_ref, g_ref, wuv_ref, lng_ref, lnb_ref, ws_ref, bs_ref, o_ref, *, tm):
    h = _rms(x_ref[...], g_ref[...]).astype(BF16)
    uv = jnp.dot(h, wuv_ref[...], preferred_element_type=F32)
    u = jax.nn.gelu(uv[:, :A_HALF])
    v = jax.nn.gelu(uv[:, A_HALF:])
    mu = jnp.mean(v, axis=-1, keepdims=True)
    var = jnp.mean(jnp.square(v - mu), axis=-1, keepdims=True)
    v = ((v - mu) * lax.rsqrt(var + EPS) * lng_ref[...] + lnb_ref[...]).astype(BF16)
    tri = (lax.broadcasted_iota(jnp.int32, (CHUNK, CHUNK), 1)
           <= lax.broadcasted_iota(jnp.int32, (CHUNK, CHUNK), 0))
    for g in range(A_GROUPS):
        ws = jnp.where(tri, ws_ref[g], 0.0).astype(BF16)
        cols = slice(g * A_GROUP_CH, (g + 1) * A_GROUP_CH)
        for c in range(tm // CHUNK):
            rows = slice(c * CHUNK, (c + 1) * CHUNK)
            mixed = jnp.dot(ws, v[rows, cols], preferred_element_type=F32) + bs_ref[g]
            o_ref[rows, cols] = (u[rows, cols] * mixed).astype(BF16)


def _gmlp(x, g, w_uv, ln_g, ln_b, w_s, b_s):
    n, d = x.shape
    tm = ROW_TM
    assert n % tm == 0 and tm % CHUNK == 0
    bs_b = jnp.broadcast_to(b_s[:, :, None], (A_GROUPS, CHUNK, A_GROUP_CH))
    return pl.pallas_call(
        functools.partial(_gmlp_kernel, tm=tm),
        grid=(n // tm,),
        in_specs=[
            pl.BlockSpec((tm, d), lambda i: (i, 0)),
            pl.BlockSpec((1, d), lambda i: (0, 0)),
            pl.BlockSpec((d, 2 * A_HALF), lambda i: (0, 0)),
            pl.BlockSpec((1, A_HALF), lambda i: (0, 0)),
            pl.BlockSpec((1, A_HALF), lambda i: (0, 0)),
            pl.BlockSpec((A_GROUPS, CHUNK, CHUNK), lambda i: (0, 0, 0)),
            pl.BlockSpec((A_GROUPS, CHUNK, A_GROUP_CH), lambda i: (0, 0, 0)),
        ],
        out_specs=pl.BlockSpec((tm, A_HALF), lambda i: (i, 0)),
        out_shape=jax.ShapeDtypeStruct((n, A_HALF), BF16),
        compiler_params=_cparams(1),
        name="gmlp",
    )(x, g.reshape(1, d), w_uv, ln_g.reshape(1, A_HALF), ln_b.reshape(1, A_HALF), w_s, bs_b)


def _rope(x, cos2, sin2):
    lane = lax.broadcasted_iota(jnp.int32, (x.shape[0], LANES), 1)
    first_half = (lane % HEAD_DIM) < (HEAD_DIM // 2)
    outs = []
    for c in range(x.shape[1] // LANES):
        xb = x[:, c * LANES:(c + 1) * LANES]
        partner = jnp.where(first_half,
                            pltpu.roll(xb, LANES - HEAD_DIM // 2, axis=1),
                            pltpu.roll(xb, HEAD_DIM // 2, axis=1))
        outs.append(xb * cos2 + partner * sin2)
    return outs[0] if len(outs) == 1 else jnp.concatenate(outs, axis=1)


def _prep_kernel(x_ref, g_ref, wqkv_ref, wqi_ref, wki_ref, wwi_ref, cos_ref, sin_ref,
                 q_ref, k_ref, vt_ref, qi_ref, ki_ref, wit_ref):
    h = _rms(x_ref[...], g_ref[...]).astype(BF16)
    cos2, sin2 = cos_ref[...], sin_ref[...]
    qkv = jnp.dot(h, wqkv_ref[...], preferred_element_type=F32)
    q_ref[...] = (_rope(qkv[:, :ATT_W], cos2, sin2) * (ATT_SCALE * LOG2_E)).astype(BF16)
    k = _rope(qkv[:, ATT_W:2 * ATT_W], cos2, sin2).astype(BF16)
    for c in range(ATT_W // LANES):
        k_ref[0, c] = k[:, c * LANES:(c + 1) * LANES]
    vt_ref[0, 0] = qkv[:, 2 * ATT_W:].T.astype(BF16)
    qi = jnp.dot(h, wqi_ref[...], preferred_element_type=F32)
    qi_ref[...] = _rope(qi, cos2, sin2).astype(BF16)
    ki2 = jnp.dot(h, wki_ref[...], preferred_element_type=F32)
    ki_ref[...] = _rope(ki2, cos2, sin2).astype(BF16)
    wi = jnp.dot(h, wwi_ref[...], preferred_element_type=F32)
    wit_ref[0] = wi.T[:IDX_HEADS, :]


def _prep(x, g, w_qkv, w_qi, w_ki2, w_wi, cos2, sin2, batch, seq):
    n, d = x.shape
    tm = ROW_TM
    assert n % tm == 0 and seq % tm == 0 and tm == ATT_BK
    tpb = seq // tm
    full = lambda i: (0, 0)
    row = lambda i: (i, 0)
    pos = lambda i: (i % tpb, 0)
    return pl.pallas_call(
        _prep_kernel,
        grid=(n // tm,),
        in_specs=[
            pl.BlockSpec((tm, d), row),
            pl.BlockSpec((1, d), full),
            pl.BlockSpec((d, 3 * ATT_W), full),
            pl.BlockSpec((d, IDX_W), full),
            pl.BlockSpec((d, LANES), full),
            pl.BlockSpec((d, LANES), full),
            pl.BlockSpec((tm, LANES), pos),
            pl.BlockSpec((tm, LANES), pos),
        ],
        out_specs=[
            pl.BlockSpec((tm, ATT_W), row),
            pl.BlockSpec((1, ATT_W // LANES, tm, LANES), lambda i: (i // tpb, 0, i % tpb, 0)),
            pl.BlockSpec((1, 1, ATT_W, tm), lambda i: (i // tpb, i % tpb, 0, 0)),
            pl.BlockSpec((tm, IDX_W), row),
            pl.BlockSpec((tm, LANES), row),
            pl.BlockSpec((1, IDX_HEADS, tm), lambda i: (i // tpb, 0, i % tpb)),
        ],
        out_shape=[
            jax.ShapeDtypeStruct((n, ATT_W), BF16),
            jax.ShapeDtypeStruct((batch, ATT_W // LANES, seq, LANES), BF16),
            jax.ShapeDtypeStruct((batch, tpb, ATT_W, tm), BF16),
            jax.ShapeDtypeStruct((n, IDX_W), BF16),
            jax.ShapeDtypeStruct((n, LANES), BF16),
            jax.ShapeDtypeStruct((batch, IDX_HEADS, seq), F32),
        ],
        compiler_params=_cparams(1),
        name="attn_prep",
    )(x, g.reshape(1, d), w_qkv, w_qi, w_ki2, w_wi, cos2, sin2)


def _attn_kernel(q_ref, qi_ref, wit_ref, k_ref, vt_ref, ki_ref, o_ref,
                 sc_ref, hi_ref, lo_ref, qp_ref, qip_ref, s_ref, p_ref, cm_ref, m_ref, l_ref, acc_ref,
                 *, bq, bk, topk, idx_bits, max_chunks):
    i = pl.program_id(1)
    nk = (i * bq + bq + bk - 1) // bk
    topk_f = float(topk)
    pair = LANES // HEAD_DIM
    n_rg = bk // RED_ROWS

    def rg(r):
        return slice(r * RED_ROWS, (r + 1) * RED_ROWS)

    def chunk(kc):
        return pl.ds(pl.multiple_of(kc * bk, bk), bk)

    lane_head = lax.broadcasted_iota(jnp.int32, (bq, LANES), 1) // HEAD_DIM
    for h in range(N_HEADS):
        slab = slice((h // pair) * LANES, (h // pair + 1) * LANES)
        keep = lane_head == (h % pair)
        qp_ref[h] = jnp.where(keep, q_ref[0, :, slab], jnp.zeros((), BF16))
        qip_ref[h] = jnp.where(keep, qi_ref[0, :, slab], jnp.zeros((), BF16))

    qpos = i * bq + lax.broadcasted_iota(jnp.int32, (RED_ROWS, bq), 1)
    krow = lax.broadcasted_iota(jnp.int32, (RED_ROWS, bq), 0)
    wit = wit_ref[0]

    def idx_body(kc, carry):
        for half in range(IDX_SPLIT):
            hrows = bk // IDX_SPLIT
            base = half * hrows
            kib = ki_ref[0, pl.ds(pl.multiple_of(kc * bk + base, hrows), hrows), :]
            acc = jnp.zeros((hrows, bq), F32)
            for h in range(IDX_HEADS):
                l = lax.dot_general(kib, qip_ref[h], NT_DIMS, preferred_element_type=F32)
                acc = acc + jnp.maximum(l, 0.0) * wit[h:h + 1, :]
            for r in range(hrows // RED_ROWS):
                rows = slice(base + r * RED_ROWS, base + (r + 1) * RED_ROWS)
                kpos = kc * bk + base + r * RED_ROWS + krow
                sc = jnp.where(kpos <= qpos, acc[rg(r)] * IDX_SCALE, -jnp.inf)
                sc_ref[kc, rows, :] = sc
                bits = lax.bitcast_convert_type(sc, jnp.int32)
                key = jnp.where(bits < 0, bits ^ np.int32(0x7FFFFFFF), bits)
                hi_ref[kc, rows, :] = (key >> 16).astype(jnp.int16)
                lo_ref[kc, rows, :] = ((key & 0xFFFF) - 32768).astype(jnp.int16)
        return carry

    lax.fori_loop(0, nk, idx_body, 0)

    one16, zero16, min16 = (jnp.full((), v, jnp.int16) for v in (1, 0, -32768))

    def select_keys(n):
        def reduce_keys(ref, fn, init, combine):
            acc = init
            for kc in range(n):
                for r in range(n_rg):
                    acc = combine(acc, fn(ref[kc, rg(r), :], kc * bk + r * RED_ROWS))
            return acc

        def count16(ref, pred):
            acc = reduce_keys(ref, lambda x, off: jnp.where(pred(x), one16, zero16),
                              jnp.zeros((RED_ROWS, bq), jnp.int16), jnp.add)
            return jnp.sum(acc.astype(jnp.int32), axis=0, keepdims=True)

        def count(pred):
            acc = reduce_keys(sc_ref, lambda s, off: jnp.where(pred(s, off), 1.0, 0.0),
                              jnp.zeros((RED_ROWS, bq), F32), jnp.add)
            return jnp.sum(acc, axis=0, keepdims=True)

        def key_min(fn):
            acc = reduce_keys(sc_ref, lambda s, off: fn(s), jnp.full((RED_ROWS, bq), jnp.inf, F32), jnp.minimum)
            return jnp.min(acc, axis=0, keepdims=True)

        def bisect16(ref, base):
            def body(it, u):
                trial = u | jnp.left_shift(jnp.int32(1), 15 - it)
                cand = (trial - 32768).astype(jnp.int16)
                cnt = base + count16(ref, lambda x: x >= cand)
                return jnp.where(cnt >= topk, trial, u)
            return lax.fori_loop(0, 16, body, jnp.zeros((1, bq), jnp.int32))

        hi_star = bisect16(hi_ref, 0) - 32768
        hi16 = hi_star.astype(jnp.int16)
        above = jnp.zeros((RED_ROWS, bq), jnp.int16)
        for kc in range(n):
            for r in range(n_rg):
                hi = hi_ref[kc, rg(r), :]
                lo_ref[kc, rg(r), :] = jnp.where(hi == hi16, lo_ref[kc, rg(r), :], min16)
                above = above + jnp.where(hi > hi16, one16, zero16)
        above = jnp.sum(above.astype(jnp.int32), axis=0, keepdims=True)
        lo_star = bisect16(lo_ref, above)
        key = jnp.maximum(hi_star * 65536 + lo_star, KEY_NEG_F32_MAX)
        thr0 = lax.bitcast_convert_type(jnp.where(key < 0, key ^ np.int32(0x7FFFFFFF), key), F32)

        t = key_min(lambda s: jnp.where(s >= thr0, s, jnp.inf))
        c_gt = count(lambda s, off: s > t)

        def adv_body(st):
            t, c_gt, _ = st
            t_next = key_min(lambda s: jnp.where(s > t, s, jnp.inf))
            t = jnp.where(c_gt >= topk_f, t_next, t)
            c_gt = count(lambda s, off: s > t)
            return t, c_gt, jnp.max(c_gt)

        t, c_gt, _ = lax.while_loop(lambda st: st[2] >= topk_f, adv_body, (t, c_gt, jnp.max(c_gt)))

        c_ge = count(lambda s, off: s >= t)
        rem = topk_f - c_gt

        def tie_split():
            def body(it, p):
                trial = p | jnp.left_shift(jnp.int32(1), idx_bits - 1 - it)
                cnt = count(lambda s, off: (s == t) & (off + krow < trial))
                return jnp.where(cnt < rem, trial, p)
            return lax.fori_loop(0, idx_bits, body, jnp.zeros((1, bq), jnp.int32))

        def mask_pass(selected):
            for kc in range(n):
                for r in range(n_rg):
                    s = sc_ref[kc, rg(r), :]
                    sc_ref[kc, rg(r), :] = jnp.where(selected(s, kc * bk + r * RED_ROWS), 0.0, -jnp.inf)

        def mask_with_ties():
            last_eq = tie_split()
            mask_pass(lambda s, off: (s > t) | ((s == t) & (off + krow <= last_eq)))

        need_split = jnp.max(c_ge - c_gt - rem) > 0.0
        lax.cond(need_split, mask_with_ties, lambda: mask_pass(lambda s, off: s >= t))

    lax.switch(nk - 1, [functools.partial(select_keys, n) for n in range(1, max_chunks + 1)])

    m_ref[...] = jnp.full(m_ref.shape, -jnp.inf, F32)
    l_ref[...] = jnp.zeros(l_ref.shape, F32)
    acc_ref[...] = jnp.zeros(acc_ref.shape, F32)

    def logits(kc, h):
        s = lax.dot_general(k_ref[0, h // pair, chunk(kc), :], qp_ref[h], NT_DIMS,
                            preferred_element_type=F32)
        mx = jnp.full((RED_ROWS, bq), -jnp.inf, F32)
        for r in range(n_rg):
            sb = s[rg(r)] + sc_ref[kc, rg(r), :]
            s_ref[h, rg(r), :] = sb
            mx = jnp.maximum(mx, sb)
        cm_ref[pl.ds(h, 1), :] = jnp.max(mx, axis=0, keepdims=True)

    def accumulate(kc, h):
        hrows = pl.ds(h * HEAD_DIM, HEAD_DIM)
        m_old = m_ref[pl.ds(h, 1), :]
        m_new = jnp.maximum(m_old, cm_ref[pl.ds(h, 1), :])
        m_use = jnp.where(m_new == -jnp.inf, 0.0, m_new)
        alpha = jnp.exp2(m_old - m_use)
        psum = jnp.zeros((RED_ROWS, bq), F32)
        for r in range(n_rg):
            p = jnp.exp2(s_ref[h, rg(r), :] - m_use)
            psum = psum + p
            p_ref[h, rg(r), :] = p.astype(BF16)
        l_ref[pl.ds(h, 1), :] = alpha * l_ref[pl.ds(h, 1), :] + jnp.sum(psum, axis=0, keepdims=True)
        pv = jnp.dot(vt_ref[0, kc, hrows, :], p_ref[h], preferred_element_type=F32)
        acc_ref[hrows, :] = alpha * acc_ref[hrows, :] + pv
        m_ref[pl.ds(h, 1), :] = m_new

    def for_heads(fn):
        for h in range(N_HEADS):
            fn(h)

    for_heads(lambda h: logits(jnp.int32(0), h))

    def att_body(kc, carry):
        def both(h):
            accumulate(kc, h)
            logits(kc + 1, h)
        for_heads(both)
        return carry

    lax.fori_loop(0, nk - 1, att_body, 0)
    for_heads(lambda h: accumulate(nk - 1, h))

    outs = [acc_ref[h * HEAD_DIM:(h + 1) * HEAD_DIM, :] / l_ref[h:h + 1, :] for h in range(N_HEADS)]
    o_ref[0] = jnp.concatenate(outs, axis=0).T.astype(BF16)


def _attention(q, qi, wit, k, vt, ki):
    b, t, _ = q.shape
    bq, bk = ATT_BQ, ATT_BK
    topk = min(TOPK_MAX, t // 4)
    assert t % bk == 0 and t % bq == 0 and bq % LANES == 0 and bk % RED_ROWS == 0 and bk >= topk
    assert (bk // RED_ROWS) * (t // bk) < 2 ** 15
    idx_bits = max(1, int(np.ceil(np.log2(t))))
    qspec = lambda w: pl.BlockSpec((1, bq, w), lambda bi, i: (bi, i, 0))
    kspec = lambda w: pl.BlockSpec((1, t, w), lambda bi, i: (bi, 0, 0))
    return pl.pallas_call(
        functools.partial(_attn_kernel, bq=bq, bk=bk, topk=topk, idx_bits=idx_bits, max_chunks=t // bk),
        grid=(b, t // bq),
        in_specs=[
            qspec(ATT_W), qspec(IDX_W),
            pl.BlockSpec((1, IDX_HEADS, bq), lambda bi, i: (bi, 0, i)),
            pl.BlockSpec((1, ATT_W // LANES, t, LANES), lambda bi, i: (bi, 0, 0, 0)),
            pl.BlockSpec((1, t // bk, ATT_W, bk), lambda bi, i: (bi, 0, 0, 0)),
            kspec(LANES),
        ],
        out_specs=qspec(ATT_W),
        out_shape=jax.ShapeDtypeStruct((b, t, ATT_W), BF16),
        scratch_shapes=[
            pltpu.VMEM((t // bk, bk, bq), F32),
            pltpu.VMEM((t // bk, bk, bq), jnp.int16),
            pltpu.VMEM((t // bk, bk, bq), jnp.int16),
            pltpu.VMEM((N_HEADS, bq, LANES), BF16),
            pltpu.VMEM((IDX_HEADS, bq, LANES), BF16),
            pltpu.VMEM((N_HEADS, bk, bq), F32),
            pltpu.VMEM((N_HEADS, bk, bq), BF16),
            pltpu.VMEM((N_HEADS, bq), F32),
            pltpu.VMEM((N_HEADS, bq), F32),
            pltpu.VMEM((N_HEADS, bq), F32),
            pltpu.VMEM((ATT_W, bq), F32),
        ],
        compiler_params=_cparams(2),
        name="dsa_attention",
    )(q, qi, wit, k, vt, ki)


def _merge_kernel(x_ref, g_ref, wg_ref, bg_ref, ya_ref, yb_ref, wpa_ref, wpb_ref, wo_ref, o_ref):
    x = x_ref[...]
    d = x.shape[1]
    h = _rms(x, g_ref[...]).astype(BF16)
    gates = jax.nn.sigmoid(jnp.dot(h, wg_ref[...], preferred_element_type=F32) + bg_ref[...])
    ya = jnp.dot(ya_ref[...], wpa_ref[...], preferred_element_type=F32)
    yb = jnp.dot(yb_ref[...], wpb_ref[...], preferred_element_type=F32)
    m = (gates[:, :d] * ya + gates[:, d:] * yb).astype(BF16)
    o_ref[...] = x + jnp.dot(m, wo_ref[...], preferred_element_type=F32)


def _merge(x, g, w_g, b_gate, ya, yb, w_pa, w_pb, w_out):
    n, d = x.shape
    tm = ROW_TM
    assert n % tm == 0
    full = lambda i: (0, 0)
    row = lambda i: (i, 0)
    return pl.pallas_call(
        _merge_kernel,
        grid=(n // tm,),
        in_specs=[
            pl.BlockSpec((tm, d), row),
            pl.BlockSpec((1, d), full),
            pl.BlockSpec((d, 2 * d), full),
            pl.BlockSpec((1, 2 * d), full),
            pl.BlockSpec((tm, A_HALF), row),
            pl.BlockSpec((tm, ATT_W), row),
            pl.BlockSpec((A_HALF, d), full),
            pl.BlockSpec((ATT_W, d), full),
            pl.BlockSpec((d, d), full),
        ],
        out_specs=pl.BlockSpec((tm, d), row),
        out_shape=jax.ShapeDtypeStruct((n, d), F32),
        compiler_params=_cparams(1),
        name="merge",
    )(x, g.reshape(1, d), w_g, b_gate.reshape(1, 2 * d), ya, yb, w_pa, w_pb, w_out)


def _rope_tables(seq):
    inv = ROPE_THETA ** (-jnp.arange(0, HEAD_DIM, 2, dtype=F32) / HEAD_DIM)
    ang = jnp.arange(seq, dtype=jnp.int32).astype(F32)[:, None] * inv[None, :]
    cos, sin = jnp.cos(ang), jnp.sin(ang)
    reps = LANES // HEAD_DIM
    cos2 = jnp.tile(jnp.concatenate([cos, cos], axis=1), (1, reps))
    sin2 = jnp.tile(jnp.concatenate([-sin, sin], axis=1), (1, reps))
    return cos2, sin2


def kernel(x, ffn1_norm, ffn1_w_gu, ffn1_w_down, mix_norm, w_in, b_gate, gmlp_ln_g, gmlp_ln_b, gmlp_w_s, gmlp_b_s, w_pa, w_pb, w_out, ffn2_norm, ffn2_w_gu, ffn2_w_down, final_norm):
    b, t, d = x.shape
    depth = ffn1_norm.shape[0]
    n = b * t
    cos2, sin2 = _rope_tables(t)
    c_uv = 2 * A_HALF
    c_qkv = c_uv + 3 * ATT_W
    c_qi = c_qkv + IDX_W
    c_ki = c_qi + IDX_DIM
    c_wi = c_ki + IDX_HEADS
    xf = x.reshape(n, d)
    for l in range(depth):
        w_in_l = w_in[l].astype(BF16)
        w_ki2 = jnp.concatenate([w_in_l[:, c_qi:c_ki]] * (LANES // IDX_DIM), axis=1)
        w_wi = jnp.pad(w_in_l[:, c_ki:c_wi], ((0, 0), (0, LANES - IDX_HEADS)))
        last = l == depth - 1
        xf = _ffn(xf, ffn1_norm[l], ffn1_w_gu[l].astype(BF16), ffn1_w_down[l].astype(BF16))
        ya = _gmlp(xf, mix_norm[l], w_in_l[:, :c_uv], gmlp_ln_g[l], gmlp_ln_b[l], gmlp_w_s[l], gmlp_b_s[l])
        q, k, vt, qi, ki, wit = _prep(xf, mix_norm[l], w_in_l[:, c_uv:c_qkv], w_in_l[:, c_qkv:c_qi],
                                      w_ki2, w_wi, cos2, sin2, b, t)
        r3 = lambda a: a.reshape(b, t, a.shape[-1])
        yb = _attention(r3(q), r3(qi), wit, k, vt, r3(ki)).reshape(n, ATT_W)
        xf = _merge(xf, mix_norm[l], w_in_l[:, c_wi:], b_gate[l], ya, yb,
                    w_pa[l].astype(BF16), w_pb[l].astype(BF16), w_out[l].astype(BF16))
        xf = _ffn(xf, ffn2_norm[l], ffn2_w_gu[l].astype(BF16), ffn2_w_down[l].astype(BF16),
                  final_g=final_norm if last else None)
    return xf.reshape(b, t, d)
```

```python
import functools

import jax
import jax.numpy as jnp
import numpy as np
from jax import lax
from jax.experimental import pallas as pl
from jax.experimental.pallas import tpu as pltpu

F32 = jnp.float32
BF16 = jnp.bfloat16

CHUNK = 128
A_GROUPS = 4
A_GROUP_CH = 128
A_HALF = A_GROUPS * A_GROUP_CH
N_HEADS = 8
HEAD_DIM = 64
ATT_W = N_HEADS * HEAD_DIM
IDX_HEADS = 8
IDX_DIM = 64
IDX_W = IDX_HEADS * IDX_DIM
TOPK_MAX = 256
ROPE_THETA = 10000.0
EPS = 1e-6
IDX_SCALE = (IDX_DIM ** -0.5) * (IDX_HEADS ** -0.5)
ATT_SCALE = HEAD_DIM ** -0.5
LOG2_E = float(np.log2(np.e))

LANES = 128
SUBLANES = 8
VMEM_LIMIT_BYTES = 56 * 1024 * 1024

FFN_TM = 512
ROW_TM = 512
ATT_BQ = 256
ATT_BK = ROW_TM
RED_ROWS = 64
IDX_SPLIT = 4

KEY_NEG_F32_MAX = np.int32(-2139095040)
F32_MAX = float(np.finfo(np.float32).max)


def _rms(x, g):
    return x * lax.rsqrt(jnp.mean(x * x, axis=-1, keepdims=True) + EPS) * g


def _cparams(n_axes):
    return pltpu.CompilerParams(
        dimension_semantics=("arbitrary",) * n_axes, vmem_limit_bytes=VMEM_LIMIT_BYTES)


def _ffn_kernel(x_ref, g_ref, wg_ref, wu_ref, wd_ref, *rest, final):
    if final:
        gf_ref, o_ref = rest
    else:
        (o_ref,) = rest
    x = x_ref[...]
    hn = _rms(x, g_ref[...]).astype(BF16)
    gate = jnp.dot(hn, wg_ref[...], preferred_element_type=F32)
    up = jnp.dot(hn, wu_ref[...], preferred_element_type=F32)
    a = (jax.nn.silu(gate) * up).astype(BF16)
    y = x + 0.5 * jnp.dot(a, wd_ref[...], preferred_element_type=F32)
    if final:
        y = _rms(y, gf_ref[...])
    o_ref[...] = y


def _ffn(x, g, w_gu, w_down, final_g=None):
    n, d = x.shape
    f = w_down.shape[0]
    tm = FFN_TM
    assert n % tm == 0 and f % LANES == 0
    final = final_g is not None
    resident = dict(pipeline_mode=pl.Buffered(1))
    in_specs = [
        pl.BlockSpec((tm, d), lambda i: (i, 0)),
        pl.BlockSpec((1, d), lambda i: (0, 0)),
        pl.BlockSpec((d, f), lambda i: (0, 0), **resident),
        pl.BlockSpec((d, f), lambda i: (0, 1), **resident),
        pl.BlockSpec((f, d), lambda i: (0, 0), **resident),
    ]
    args = [x, g.reshape(1, d), w_gu, w_gu, w_down]
    if final:
        in_specs.append(pl.BlockSpec((1, d), lambda i: (0, 0)))
        args.append(final_g.reshape(1, d))
    return pl.pallas_call(
        functools.partial(_ffn_kernel, final=final),
        grid=(n // tm,),
        in_specs=in_specs,
        out_specs=pl.BlockSpec((tm, d), lambda i: (i, 0)),
        out_shape=jax.ShapeDtypeStruct((n, d), F32),
        compiler_params=_cparams(1),
        name="ffn",
    )(*args)


def _gmlp_kernel(x_ref, g_ref, wuv_ref, lng_ref, lnb_ref, ws_ref, bs_ref, o_ref, *, tm):
    h = _rms(x_ref[...], g_ref[...]).astype(BF16)
    uv = jnp.dot(h, wuv_ref[...], preferred_element_type=F32)
    u = jax.nn.gelu(uv[:, :A_HALF])
    v = jax.nn.gelu(uv[:, A_HALF:])
    mu = jnp.mean(v, axis=-1, keepdims=True)
    var = jnp.mean(jnp.square(v - mu), axis=-1, keepdims=True)
    v = ((v - mu) * lax.rsqrt(var + EPS) * lng_ref[...] + lnb_ref[...]).astype(BF16)
    tri = (lax.broadcasted_iota(jnp.int32, (CHUNK, CHUNK), 1)
           <= lax.broadcasted_iota(jnp.int32, (CHUNK, CHUNK), 0))
    for g in range(A_GROUPS):
        ws = jnp.where(tri, ws_ref[g], 0.0).astype(BF16)
        cols = slice(g * A_GROUP_CH, (g + 1) * A_GROUP_CH)
        for c in range(tm // CHUNK):
            rows = slice(c * CHUNK, (c + 1) * CHUNK)
            mixed = jnp.dot(ws, v[rows, cols], preferred_element_type=F32) + bs_ref[g]
            o_ref[rows, cols] = (u[rows, cols] * mixed).astype(BF16)


def _gmlp(x, g, w_uv, ln_g, ln_b, w_s, b_s):
    n, d = x.shape
    tm = ROW_TM
    assert n % tm == 0 and tm % CHUNK == 0
    bs_b = jnp.broadcast_to(b_s[:, :, None], (A_GROUPS, CHUNK, A_GROUP_CH))
    return pl.pallas_call(
        functools.partial(_gmlp_kernel, tm=tm),
        grid=(n // tm,),
        in_specs=[
            pl.BlockSpec((tm, d), lambda i: (i, 0)),
            pl.BlockSpec((1, d), lambda i: (0, 0)),
            pl.BlockSpec((d, 2 * A_HALF), lambda i: (0, 0)),
            pl.BlockSpec((1, A_HALF), lambda i: (0, 0)),
            pl.BlockSpec((1, A_HALF), lambda i: (0, 0)),
            pl.BlockSpec((A_GROUPS, CHUNK, CHUNK), lambda i: (0, 0, 0)),
            pl.BlockSpec((A_GROUPS, CHUNK, A_GROUP_CH), lambda i: (0, 0, 0)),
        ],
        out_specs=pl.BlockSpec((tm, A_HALF), lambda i: (i, 0)),
        out_shape=jax.ShapeDtypeStruct((n, A_HALF), BF16),
        compiler_params=_cparams(1),
        name="gmlp",
    )(x, g.reshape(1, d), w_uv, ln_g.reshape(1, A_HALF), ln_b.reshape(1, A_HALF), w_s, bs_b)


def _rope(x, cos2, sin2):
    lane = lax.broadcasted_iota(jnp.int32, (x.shape[0], LANES), 1)
    first_half = (lane % HEAD_DIM) < (HEAD_DIM // 2)
    outs = []
    for c in range(x.shape[1] // LANES):
        xb = x[:, c * LANES:(c + 1) * LANES]
        partner = jnp.where(first_half,
                            pltpu.roll(xb, LANES - HEAD_DIM // 2, axis=1),
                            pltpu.roll(xb, HEAD_DIM // 2, axis=1))
        outs.append(xb * cos2 + partner * sin2)
    return outs[0] if len(outs) == 1 else jnp.concatenate(outs, axis=1)


def _prep_kernel(x_ref, g_ref, wqkv_ref, wqi_ref, wki_ref, wwi_ref, cos_ref, sin_ref,
                 qt_ref, k_ref, vt_ref, qit_ref, ki_ref, wit_ref):
    h = _rms(x_ref[...], g_ref[...]).astype(BF16)
    cos2, sin2 = cos_ref[...], sin_ref[...]
    qkv = jnp.dot(h, wqkv_ref[...], preferred_element_type=F32)
    qt_ref[0] = (_rope(qkv[:, :ATT_W], cos2, sin2) * (ATT_SCALE * LOG2_E)).T.astype(BF16)
    k = _rope(qkv[:, ATT_W:2 * ATT_W], cos2, sin2).astype(BF16)
    for c in range(ATT_W // LANES):
        k_ref[0, c] = k[:, c * LANES:(c + 1) * LANES]
    vt_ref[0, 0] = qkv[:, 2 * ATT_W:].T.astype(BF16)
    qi = jnp.dot(h, wqi_ref[...], preferred_element_type=F32)
    qit_ref[0] = _rope(qi, cos2, sin2).T.astype(BF16)
    ki2 = jnp.dot(h, wki_ref[...], preferred_element_type=F32)
    ki_ref[...] = _rope(ki2, cos2, sin2).astype(BF16)
    wi = jnp.dot(h, wwi_ref[...], preferred_element_type=F32)
    wit_ref[0] = wi.T[:IDX_HEADS, :]


def _prep(x, g, w_qkv, w_qi, w_ki2, w_wi, cos2, sin2, batch, seq):
    n, d = x.shape
    tm = ROW_TM
    assert n % tm == 0 and seq % tm == 0 and tm == ATT_BK
    tpb = seq // tm
    full = lambda i: (0, 0)
    row = lambda i: (i, 0)
    pos = lambda i: (i % tpb, 0)
    return pl.pallas_call(
        _prep_kernel,
        grid=(n // tm,),
        in_specs=[
            pl.BlockSpec((tm, d), row),
            pl.BlockSpec((1, d), full),
            pl.BlockSpec((d, 3 * ATT_W), full),
            pl.BlockSpec((d, IDX_W), full),
            pl.BlockSpec((d, LANES), full),
            pl.BlockSpec((d, LANES), full),
            pl.BlockSpec((tm, LANES), pos),
            pl.BlockSpec((tm, LANES), pos),
        ],
        out_specs=[
            pl.BlockSpec((1, ATT_W, tm), lambda i: (i // tpb, 0, i % tpb)),
            pl.BlockSpec((1, ATT_W // LANES, tm, LANES), lambda i: (i // tpb, 0, i % tpb, 0)),
            pl.BlockSpec((1, 1, ATT_W, tm), lambda i: (i // tpb, i % tpb, 0, 0)),
            pl.BlockSpec((1, IDX_W, tm), lambda i: (i // tpb, 0, i % tpb)),
            pl.BlockSpec((tm, LANES), row),
            pl.BlockSpec((1, IDX_HEADS, tm), lambda i: (i // tpb, 0, i % tpb)),
        ],
        out_shape=[
            jax.ShapeDtypeStruct((batch, ATT_W, seq), BF16),
            jax.ShapeDtypeStruct((batch, ATT_W // LANES, seq, LANES), BF16),
            jax.ShapeDtypeStruct((batch, tpb, ATT_W, tm), BF16),
            jax.ShapeDtypeStruct((batch, IDX_W, seq), BF16),
            jax.ShapeDtypeStruct((n, LANES), BF16),
            jax.ShapeDtypeStruct((batch, IDX_HEADS, seq), F32),
        ],
        compiler_params=_cparams(1),
        name="attn_prep",
    )(x, g.reshape(1, d), w_qkv, w_qi, w_ki2, w_wi, cos2, sin2)


def _attn_kernel(qt_ref, qit_ref, wit_ref, k_ref, vt_ref, ki_ref, o_ref,
                 sc_ref, hi_ref, lo_ref, qp_ref, qip_ref, s_ref, p_ref, cm_ref, m_ref, l_ref, acc_ref,
                 *, bq, bk, topk, idx_bits, max_chunks):
    i = pl.program_id(1)
    nk = (i * bq + bq + bk - 1) // bk
    topk_f = float(topk)
    pair = LANES // HEAD_DIM
    n_rg = bk // RED_ROWS

    def rg(r):
        return slice(r * RED_ROWS, (r + 1) * RED_ROWS)

    def chunk(kc):
        return pl.ds(pl.multiple_of(kc * bk, bk), bk)

    row_head = lax.broadcasted_iota(jnp.int32, (LANES, bq), 0) // HEAD_DIM
    for h in range(N_HEADS):
        slab = slice((h // pair) * LANES, (h // pair + 1) * LANES)
        keep = row_head == (h % pair)
        qp_ref[h] = jnp.where(keep, qt_ref[0, slab, :], jnp.zeros((), BF16))
        qip_ref[h] = jnp.where(keep, qit_ref[0, slab, :], jnp.zeros((), BF16))

    qpos = i * bq + lax.broadcasted_iota(jnp.int32, (RED_ROWS, bq), 1)
    krow = lax.broadcasted_iota(jnp.int32, (RED_ROWS, bq), 0)
    wit = wit_ref[0]

    def idx_body(kc, carry):
        for half in range(IDX_SPLIT):
            hrows = bk // IDX_SPLIT
            base = half * hrows
            kib = ki_ref[0, pl.ds(pl.multiple_of(kc * bk + base, hrows), hrows), :]
            acc = jnp.zeros((hrows, bq), F32)
            for h in range(IDX_HEADS):
                l = jnp.dot(kib, qip_ref[h], preferred_element_type=F32)
                acc = acc + jnp.maximum(l, 0.0) * wit[h:h + 1, :]
            for r in range(hrows // RED_ROWS):
                rows = slice(base + r * RED_ROWS, base + (r + 1) * RED_ROWS)
                kpos = kc * bk + base + r * RED_ROWS + krow
                sc = jnp.where(kpos <= qpos, acc[rg(r)] * IDX_SCALE, -jnp.inf)
                sc_ref[kc, rows, :] = sc
                bits = lax.bitcast_convert_type(sc, jnp.int32)
                key = jnp.where(bits < 0, bits ^ np.int32(0x7FFFFFFF), bits)
                hi_ref[kc, rows, :] = (key >> 16).astype(jnp.int16)
                lo_ref[kc, rows, :] = ((key & 0xFFFF) - 32768).astype(jnp.int16)
        return carry

    lax.fori_loop(0, nk, idx_body, 0)

    one16, zero16, min16 = (jnp.full((), v, jnp.int16) for v in (1, 0, -32768))

    def select_keys(n):
        def reduce_keys(ref, fn, init, combine):
            acc = init
            for kc in range(n):
                for r in range(n_rg):
                    acc = combine(acc, fn(ref[kc, rg(r), :], kc * bk + r * RED_ROWS))
            return acc

        def count16(ref, pred):
            acc = reduce_keys(ref, lambda x, off: jnp.where(pred(x), one16, zero16),
                              jnp.zeros((RED_ROWS, bq), jnp.int16), jnp.add)
            return jnp.sum(acc.astype(jnp.int32), axis=0, keepdims=True)

        def count(pred):
            acc = reduce_keys(sc_ref, lambda s, off: jnp.where(pred(s, off), 1.0, 0.0),
                              jnp.zeros((RED_ROWS, bq), F32), jnp.add)
            return jnp.sum(acc, axis=0, keepdims=True)

        def key_min(fn):
            acc = reduce_keys(sc_ref, lambda s, off: fn(s), jnp.full((RED_ROWS, bq), jnp.inf, F32), jnp.minimum)
            return jnp.min(acc, axis=0, keepdims=True)

        def bisect16(ref, base):
            def body(it, u):
                trial = u | jnp.left_shift(jnp.int32(1), 15 - it)
                cand = (trial - 32768).astype(jnp.int16)
                cnt = base + count16(ref, lambda x: x >= cand)
                return jnp.where(cnt >= topk, trial, u)
            return lax.fori_loop(0, 16, body, jnp.zeros((1, bq), jnp.int32))

        hi_star = bisect16(hi_ref, 0) - 32768
        hi16 = hi_star.astype(jnp.int16)
        above = jnp.zeros((RED_ROWS, bq), jnp.int16)
        for kc in range(n):
            for r in range(n_rg):
                hi = hi_ref[kc, rg(r), :]
                lo_ref[kc, rg(r), :] = jnp.where(hi == hi16, lo_ref[kc, rg(r), :], min16)
                above = above + jnp.where(hi > hi16, one16, zero16)
        above = jnp.sum(above.astype(jnp.int32), axis=0, keepdims=True)
        lo_star = bisect16(lo_ref, above)
        key = jnp.maximum(hi_star * 65536 + lo_star, KEY_NEG_F32_MAX)
        thr0 = lax.bitcast_convert_type(jnp.where(key < 0, key ^ np.int32(0x7FFFFFFF), key), F32)

        t = key_min(lambda s: jnp.where(s >= thr0, s, jnp.inf))
        c_gt = count(lambda s, off: s > t)

        def adv_body(st):
            t, c_gt, _ = st
            t_next = key_min(lambda s: jnp.where(s > t, s, jnp.inf))
            t = jnp.where(c_gt >= topk_f, t_next, t)
            c_gt = count(lambda s, off: s > t)
            return t, c_gt, jnp.max(c_gt)

        t, c_gt, _ = lax.while_loop(lambda st: st[2] >= topk_f, adv_body, (t, c_gt, jnp.max(c_gt)))

        c_ge = count(lambda s, off: s >= t)
        rem = topk_f - c_gt

        def tie_split():
            def body(it, p):
                trial = p | jnp.left_shift(jnp.int32(1), idx_bits - 1 - it)
                cnt = count(lambda s, off: (s == t) & (off + krow < trial))
                return jnp.where(cnt < rem, trial, p)
            return lax.fori_loop(0, idx_bits, body, jnp.zeros((1, bq), jnp.int32))

        def mask_pass(selected):
            for kc in range(n):
                for r in range(n_rg):
                    s = sc_ref[kc, rg(r), :]
                    sc_ref[kc, rg(r), :] = jnp.where(selected(s, kc * bk + r * RED_ROWS), 0.0, -jnp.inf)

        def mask_with_ties():
            last_eq = tie_split()
            mask_pass(lambda s, off: (s > t) | ((s == t) & (off + krow <= last_eq)))

        need_split = jnp.max(c_ge - c_gt - rem) > 0.0
        lax.cond(need_split, mask_with_ties, lambda: mask_pass(lambda s, off: s >= t))

    lax.switch(nk - 1, [functools.partial(select_keys, n) for n in range(1, max_chunks + 1)])

    m_ref[...] = jnp.full(m_ref.shape, -jnp.inf, F32)
    l_ref[...] = jnp.zeros(l_ref.shape, F32)
    acc_ref[...] = jnp.zeros(acc_ref.shape, F32)

    def logits(kc, h):
        s = jnp.dot(k_ref[0, h // pair, chunk(kc), :], qp_ref[h], preferred_element_type=F32)
        mx = jnp.full((RED_ROWS, bq), -jnp.inf, F32)
        for r in range(n_rg):
            sb = s[rg(r)] + sc_ref[kc, rg(r), :]
            s_ref[h, rg(r), :] = sb
            mx = jnp.maximum(mx, sb)
        cm_ref[pl.ds(h, 1), :] = jnp.max(mx, axis=0, keepdims=True)

    def accumulate(kc, h):
        hrows = pl.ds(h * HEAD_DIM, HEAD_DIM)
        m_old = m_ref[pl.ds(h, 1), :]
        m_new = jnp.maximum(m_old, cm_ref[pl.ds(h, 1), :])
        m_use = jnp.where(m_new == -jnp.inf, 0.0, m_new)
        alpha = jnp.exp2(m_old - m_use)
        psum = jnp.zeros((RED_ROWS, bq), F32)
        for r in range(n_rg):
            p = jnp.exp2(s_ref[h, rg(r), :] - m_use)
            psum = psum + p
            p_ref[h, rg(r), :] = p.astype(BF16)
        l_ref[pl.ds(h, 1), :] = alpha * l_ref[pl.ds(h, 1), :] + jnp.sum(psum, axis=0, keepdims=True)
        pv = jnp.dot(vt_ref[0, kc, hrows, :], p_ref[h], preferred_element_type=F32)
        acc_ref[hrows, :] = alpha * acc_ref[hrows, :] + pv
        m_ref[pl.ds(h, 1), :] = m_new

    def for_heads(fn):
        for h in range(N_HEADS):
            fn(h)

    for_heads(lambda h: logits(jnp.int32(0), h))

    def att_body(kc, carry):
        def both(h):
            accumulate(kc, h)
            logits(kc + 1, h)
        for_heads(both)
        return carry

    lax.fori_loop(0, nk - 1, att_body, 0)
    for_heads(lambda h: accumulate(nk - 1, h))

    outs = [acc_ref[h * HEAD_DIM:(h + 1) * HEAD_DIM, :] / l_ref[h:h + 1, :] for h in range(N_HEADS)]
    o_ref[0] = jnp.concatenate(outs, axis=0).T.astype(BF16)


def _attention(qt, qit, wit, k, vt, ki):
    b, _, t = qt.shape
    bq, bk = ATT_BQ, ATT_BK
    topk = min(TOPK_MAX, t // 4)
    assert t % bk == 0 and t % bq == 0 and bq % LANES == 0 and bk % RED_ROWS == 0 and bk >= topk
    assert (bk // RED_ROWS) * (t // bk) < 2 ** 15
    idx_bits = max(1, int(np.ceil(np.log2(t))))
    qspec = lambda w: pl.BlockSpec((1, bq, w), lambda bi, i: (bi, i, 0))
    qtspec = lambda w: pl.BlockSpec((1, w, bq), lambda bi, i: (bi, 0, i))
    kspec = lambda w: pl.BlockSpec((1, t, w), lambda bi, i: (bi, 0, 0))
    return pl.pallas_call(
        functools.partial(_attn_kernel, bq=bq, bk=bk, topk=topk, idx_bits=idx_bits, max_chunks=t // bk),
        grid=(b, t // bq),
        in_specs=[
            qtspec(ATT_W), qtspec(IDX_W),
            pl.BlockSpec((1, IDX_HEADS, bq), lambda bi, i: (bi, 0, i)),
            pl.BlockSpec((1, ATT_W // LANES, t, LANES), lambda bi, i: (bi, 0, 0, 0)),
            pl.BlockSpec((1, t // bk, ATT_W, bk), lambda bi, i: (bi, 0, 0, 0)),
            kspec(LANES),
        ],
        out_specs=qspec(ATT_W),
        out_shape=jax.ShapeDtypeStruct((b, t, ATT_W), BF16),
        scratch_shapes=[
            pltpu.VMEM((t // bk, bk, bq), F32),
            pltpu.VMEM((t // bk, bk, bq), jnp.int16),
            pltpu.VMEM((t // bk, bk, bq), jnp.int16),
            pltpu.VMEM((N_HEADS, LANES, bq), BF16),
            pltpu.VMEM((IDX_HEADS, LANES, bq), BF16),
            pltpu.VMEM((N_HEADS, bk, bq), F32),
            pltpu.VMEM((N_HEADS, bk, bq), BF16),
            pltpu.VMEM((N_HEADS, bq), F32),
            pltpu.VMEM((N_HEADS, bq), F32),
            pltpu.VMEM((N_HEADS, bq), F32),
            pltpu.VMEM((ATT_W, bq), F32),
        ],
        compiler_params=_cparams(2),
        name="dsa_attention",
    )(qt, qit, wit, k, vt, ki)


def _merge_kernel(x_ref, g_ref, wg_ref, bg_ref, ya_ref, yb_ref, wpa_ref, wpb_ref, wo_ref, o_ref):
    x = x_ref[...]
    d = x.shape[1]
    h = _rms(x, g_ref[...]).astype(BF16)
    gates = jax.nn.sigmoid(jnp.dot(h, wg_ref[...], preferred_element_type=F32) + bg_ref[...])
    ya = jnp.dot(ya_ref[...], wpa_ref[...], preferred_element_type=F32)
    yb = jnp.dot(yb_ref[...], wpb_ref[...], preferred_element_type=F32)
    m = (gates[:, :d] * ya + gates[:, d:] * yb).astype(BF16)
    o_ref[...] = x + jnp.dot(m, wo_ref[...], preferred_element_type=F32)


def _merge(x, g, w_g, b_gate, ya, yb, w_pa, w_pb, w_out):
    n, d = x.shape
    tm = ROW_TM
    assert n % tm == 0
    full = lambda i: (0, 0)
    row = lambda i: (i, 0)
    return pl.pallas_call(
        _merge_kernel,
        grid=(n // tm,),
        in_specs=[
            pl.BlockSpec((tm, d), row),
            pl.BlockSpec((1, d), full),
            pl.BlockSpec((d, 2 * d), full),
            pl.BlockSpec((1, 2 * d), full),
            pl.BlockSpec((tm, A_HALF), row),
            pl.BlockSpec((tm, ATT_W), row),
            pl.BlockSpec((A_HALF, d), full),
            pl.BlockSpec((ATT_W, d), full),
            pl.BlockSpec((d, d), full),
        ],
        out_specs=pl.BlockSpec((tm, d), row),
        out_shape=jax.ShapeDtypeStruct((n, d), F32),
        compiler_params=_cparams(1),
        name="merge",
    )(x, g.reshape(1, d), w_g, b_gate.reshape(1, 2 * d), ya, yb, w_pa, w_pb, w_out)


def _rope_tables(seq):
    inv = ROPE_THETA ** (-jnp.arange(0, HEAD_DIM, 2, dtype=F32) / HEAD_DIM)
    ang = jnp.arange(seq, dtype=jnp.int32).astype(F32)[:, None] * inv[None, :]
    cos, sin = jnp.cos(ang), jnp.sin(ang)
    reps = LANES // HEAD_DIM
    cos2 = jnp.tile(jnp.concatenate([cos, cos], axis=1), (1, reps))
    sin2 = jnp.tile(jnp.concatenate([-sin, sin], axis=1), (1, reps))
    return cos2, sin2


def kernel(x, ffn1_norm, ffn1_w_gu, ffn1_w_down, mix_norm, w_in, b_gate, gmlp_ln_g, gmlp_ln_b, gmlp_w_s, gmlp_b_s, w_pa, w_pb, w_out, ffn2_norm, ffn2_w_gu, ffn2_w_down, final_norm):
    b, t, d = x.shape
    depth = ffn1_norm.shape[0]
    n = b * t
    cos2, sin2 = _rope_tables(t)
    c_uv = 2 * A_HALF
    c_qkv = c_uv + 3 * ATT_W
    c_qi = c_qkv + IDX_W
    c_ki = c_qi + IDX_DIM
    c_wi = c_ki + IDX_HEADS
    xf = x.reshape(n, d)
    for l in range(depth):
        w_in_l = w_in[l].astype(BF16)
        w_ki2 = jnp.concatenate([w_in_l[:, c_qi:c_ki]] * (LANES // IDX_DIM), axis=1)
        w_wi = jnp.pad(w_in_l[:, c_ki:c_wi], ((0, 0), (0, LANES - IDX_HEADS)))
        last = l == depth - 1
        xf = _ffn(xf, ffn1_norm[l], ffn1_w_gu[l].astype(BF16), ffn1_w_down[l].astype(BF16))
        ya = _gmlp(xf, mix_norm[l], w_in_l[:, :c_uv], gmlp_ln_g[l], gmlp_ln_b[l], gmlp_w_s[l], gmlp_b_s[l])
        qt, k, vt, qit, ki, wit = _prep(xf, mix_norm[l], w_in_l[:, c_uv:c_qkv], w_in_l[:, c_qkv:c_qi],
                                      w_ki2, w_wi, cos2, sin2, b, t)
        r3 = lambda a: a.reshape(b, t, a.shape[-1])
        yb = _attention(qt, qit, wit, k, vt, r3(ki)).reshape(n, ATT_W)
        xf = _merge(xf, mix_norm[l], w_in_l[:, c_wi:], b_gate[l], ya, yb,
                    w_pa[l].astype(BF16), w_pb[l].astype(BF16), w_out[l].astype(BF16))
        xf = _ffn(xf, ffn2_norm[l], ffn2_w_gu[l].astype(BF16), ffn2_w_down[l].astype(BF16),
                  final_g=final_norm if last else None)
    return xf.reshape(b, t, d)
```

```python
import functools

import jax
import jax.numpy as jnp
import numpy as np
from jax import lax
from jax.experimental import pallas as pl
from jax.experimental.pallas import tpu as pltpu

F32 = jnp.float32
BF16 = jnp.bfloat16

CHUNK = 128
A_GROUPS = 4
A_GROUP_CH = 128
A_HALF = A_GROUPS * A_GROUP_CH
N_HEADS = 8
HEAD_DIM = 64
ATT_W = N_HEADS * HEAD_DIM
IDX_HEADS = 8
IDX_DIM = 64
IDX_W = IDX_HEADS * IDX_DIM
TOPK_MAX = 256
ROPE_THETA = 10000.0
EPS = 1e-6
IDX_SCALE = (IDX_DIM ** -0.5) * (IDX_HEADS ** -0.5)
ATT_SCALE = HEAD_DIM ** -0.5
LOG2_E = float(np.log2(np.e))

LANES = 128
SUBLANES = 8
VMEM_LIMIT_BYTES = 56 * 1024 * 1024

FFN_TM = 512
ROW_TM = 512
ATT_BQ = 256
ATT_BK = ROW_TM
RED_ROWS = 64
IDX_SPLIT = 8
VT_ROWS = 80

KEY_NEG_F32_MAX = np.int32(-2139095040)
F32_MAX = float(np.finfo(np.float32).max)


def _rms(x, g):
    return x * lax.rsqrt(jnp.mean(x * x, axis=-1, keepdims=True) + EPS) * g


def _cparams(n_axes):
    return pltpu.CompilerParams(
        dimension_semantics=("arbitrary",) * n_axes, vmem_limit_bytes=VMEM_LIMIT_BYTES)


def _ffn_kernel(x_ref, g_ref, wg_ref, wu_ref, wd_ref, *rest, final):
    if final:
        gf_ref, o_ref = rest
    else:
        (o_ref,) = rest
    x = x_ref[...]
    hn = _rms(x, g_ref[...]).astype(BF16)
    gate = jnp.dot(hn, wg_ref[...], preferred_element_type=F32)
    up = jnp.dot(hn, wu_ref[...], preferred_element_type=F32)
    a = (jax.nn.silu(gate) * up).astype(BF16)
    y = x + 0.5 * jnp.dot(a, wd_ref[...], preferred_element_type=F32)
    if final:
        y = _rms(y, gf_ref[...])
    o_ref[...] = y


def _ffn(x, g, w_gu, w_down, final_g=None):
    n, d = x.shape
    f = w_down.shape[0]
    tm = FFN_TM
    assert n % tm == 0 and f % LANES == 0
    final = final_g is not None
    resident = dict(pipeline_mode=pl.Buffered(1))
    in_specs = [
        pl.BlockSpec((tm, d), lambda i: (i, 0)),
        pl.BlockSpec((1, d), lambda i: (0, 0)),
        pl.BlockSpec((d, f), lambda i: (0, 0), **resident),
        pl.BlockSpec((d, f), lambda i: (0, 1), **resident),
        pl.BlockSpec((f, d), lambda i: (0, 0), **resident),
    ]
    args = [x, g.reshape(1, d), w_gu, w_gu, w_down]
    if final:
        in_specs.append(pl.BlockSpec((1, d), lambda i: (0, 0)))
        args.append(final_g.reshape(1, d))
    return pl.pallas_call(
        functools.partial(_ffn_kernel, final=final),
        grid=(n // tm,),
        in_specs=in_specs,
        out_specs=pl.BlockSpec((tm, d), lambda i: (i, 0)),
        out_shape=jax.ShapeDtypeStruct((n, d), F32),
        compiler_params=_cparams(1),
        name="ffn",
    )(*args)


def _gmlp_kernel(x_ref, g_ref, wuv_ref, lng_ref, lnb_ref, ws_ref, bs_ref, o_ref, *, tm):
    h = _rms(x_ref[...], g_ref[...]).astype(BF16)
    uv = jnp.dot(h, wuv_ref[...], preferred_element_type=F32)
    u = jax.nn.gelu(uv[:, :A_HALF])
    v = jax.nn.gelu(uv[:, A_HALF:])
    mu = jnp.mean(v, axis=-1, keepdims=True)
    var = jnp.mean(jnp.square(v - mu), axis=-1, keepdims=True)
    v = ((v - mu) * lax.rsqrt(var + EPS) * lng_ref[...] + lnb_ref[...]).astype(BF16)
    tri = (lax.broadcasted_iota(jnp.int32, (CHUNK, CHUNK), 1)
           <= lax.broadcasted_iota(jnp.int32, (CHUNK, CHUNK), 0))
    for g in range(A_GROUPS):
        ws = jnp.where(tri, ws_ref[g], 0.0).astype(BF16)
        cols = slice(g * A_GROUP_CH, (g + 1) * A_GROUP_CH)
        for c in range(tm // CHUNK):
            rows = slice(c * CHUNK, (c + 1) * CHUNK)
            mixed = jnp.dot(ws, v[rows, cols], preferred_element_type=F32) + bs_ref[g]
            o_ref[rows, cols] = (u[rows, cols] * mixed).astype(BF16)


def _gmlp(x, g, w_uv, ln_g, ln_b, w_s, b_s):
    n, d = x.shape
    tm = ROW_TM
    assert n % tm == 0 and tm % CHUNK == 0
    bs_b = jnp.broadcast_to(b_s[:, :, None], (A_GROUPS, CHUNK, A_GROUP_CH))
    return pl.pallas_call(
        functools.partial(_gmlp_kernel, tm=tm),
        grid=(n // tm,),
        in_specs=[
            pl.BlockSpec((tm, d), lambda i: (i, 0)),
            pl.BlockSpec((1, d), lambda i: (0, 0)),
            pl.BlockSpec((d, 2 * A_HALF), lambda i: (0, 0)),
            pl.BlockSpec((1, A_HALF), lambda i: (0, 0)),
            pl.BlockSpec((1, A_HALF), lambda i: (0, 0)),
            pl.BlockSpec((A_GROUPS, CHUNK, CHUNK), lambda i: (0, 0, 0)),
            pl.BlockSpec((A_GROUPS, CHUNK, A_GROUP_CH), lambda i: (0, 0, 0)),
        ],
        out_specs=pl.BlockSpec((tm, A_HALF), lambda i: (i, 0)),
        out_shape=jax.ShapeDtypeStruct((n, A_HALF), BF16),
        compiler_params=_cparams(1),
        name="gmlp",
    )(x, g.reshape(1, d), w_uv, ln_g.reshape(1, A_HALF), ln_b.reshape(1, A_HALF), w_s, bs_b)


def _rope(x, cos2, sin2):
    lane = lax.broadcasted_iota(jnp.int32, (x.shape[0], LANES), 1)
    first_half = (lane % HEAD_DIM) < (HEAD_DIM // 2)
    outs = []
    for c in range(x.shape[1] // LANES):
        xb = x[:, c * LANES:(c + 1) * LANES]
        partner = jnp.where(first_half,
                            pltpu.roll(xb, LANES - HEAD_DIM // 2, axis=1),
                            pltpu.roll(xb, HEAD_DIM // 2, axis=1))
        outs.append(xb * cos2 + partner * sin2)
    return outs[0] if len(outs) == 1 else jnp.concatenate(outs, axis=1)


def _prep_kernel(x_ref, g_ref, wqkv_ref, wqi_ref, wki_ref, wwi_ref, cos_ref, sin_ref,
                 qt_ref, k_ref, vt_ref, qit_ref, ki_ref, wit_ref):
    h = _rms(x_ref[...], g_ref[...]).astype(BF16)
    cos2, sin2 = cos_ref[...], sin_ref[...]
    qkv = jnp.dot(h, wqkv_ref[...], preferred_element_type=F32)
    qt_ref[0] = (_rope(qkv[:, :ATT_W], cos2, sin2) * (ATT_SCALE * LOG2_E)).T.astype(BF16)
    k = _rope(qkv[:, ATT_W:2 * ATT_W], cos2, sin2).astype(BF16)
    for c in range(ATT_W // LANES):
        k_ref[0, c] = k[:, c * LANES:(c + 1) * LANES]
    vt = qkv[:, 2 * ATT_W:].T
    tm = vt.shape[1]
    ones_rows = (lax.broadcasted_iota(jnp.int32, (VT_ROWS - HEAD_DIM, tm), 0) == 0).astype(F32)
    for hd in range(N_HEADS):
        vh = jnp.concatenate([vt[hd * HEAD_DIM:(hd + 1) * HEAD_DIM], ones_rows], axis=0)
        vt_ref[0, 0, hd * VT_ROWS:(hd + 1) * VT_ROWS, :] = vh.astype(BF16)
    qi = jnp.dot(h, wqi_ref[...], preferred_element_type=F32)
    qit_ref[0] = _rope(qi, cos2, sin2).T.astype(BF16)
    ki2 = jnp.dot(h, wki_ref[...], preferred_element_type=F32)
    ki_ref[...] = _rope(ki2, cos2, sin2).astype(BF16)
    wi = jnp.dot(h, wwi_ref[...], preferred_element_type=F32)
    wit_ref[0] = wi.T[:IDX_HEADS, :]


def _prep(x, g, w_qkv, w_qi, w_ki2, w_wi, cos2, sin2, batch, seq):
    n, d = x.shape
    tm = ROW_TM
    assert n % tm == 0 and seq % tm == 0 and tm == ATT_BK
    tpb = seq // tm
    full = lambda i: (0, 0)
    row = lambda i: (i, 0)
    pos = lambda i: (i % tpb, 0)
    return pl.pallas_call(
        _prep_kernel,
        grid=(n // tm,),
        in_specs=[
            pl.BlockSpec((tm, d), row),
            pl.BlockSpec((1, d), full),
            pl.BlockSpec((d, 3 * ATT_W), full),
            pl.BlockSpec((d, IDX_W), full),
            pl.BlockSpec((d, LANES), full),
            pl.BlockSpec((d, LANES), full),
            pl.BlockSpec((tm, LANES), pos),
            pl.BlockSpec((tm, LANES), pos),
        ],
        out_specs=[
            pl.BlockSpec((1, ATT_W, tm), lambda i: (i // tpb, 0, i % tpb)),
            pl.BlockSpec((1, ATT_W // LANES, tm, LANES), lambda i: (i // tpb, 0, i % tpb, 0)),
            pl.BlockSpec((1, 1, N_HEADS * VT_ROWS, tm), lambda i: (i // tpb, i % tpb, 0, 0)),
            pl.BlockSpec((1, IDX_W, tm), lambda i: (i // tpb, 0, i % tpb)),
            pl.BlockSpec((tm, LANES), row),
            pl.BlockSpec((1, IDX_HEADS, tm), lambda i: (i // tpb, 0, i % tpb)),
        ],
        out_shape=[
            jax.ShapeDtypeStruct((batch, ATT_W, seq), BF16),
            jax.ShapeDtypeStruct((batch, ATT_W // LANES, seq, LANES), BF16),
            jax.ShapeDtypeStruct((batch, tpb, N_HEADS * VT_ROWS, tm), BF16),
            jax.ShapeDtypeStruct((batch, IDX_W, seq), BF16),
            jax.ShapeDtypeStruct((n, LANES), BF16),
            jax.ShapeDtypeStruct((batch, IDX_HEADS, seq), F32),
        ],
        compiler_params=_cparams(1),
        name="attn_prep",
    )(x, g.reshape(1, d), w_qkv, w_qi, w_ki2, w_wi, cos2, sin2)


def _attn_kernel(qt_ref, qit_ref, wit_ref, k_ref, vt_ref, ki_ref, o_ref,
                 sc_ref, hi_ref, lo_ref, qp_ref, qip_ref, s_ref, p_ref, cm_ref, m_ref, acc_ref,
                 *, bq, bk, topk, idx_bits, max_chunks):
    i = pl.program_id(1)
    nk = (i * bq + bq + bk - 1) // bk
    topk_f = float(topk)
    pair = LANES // HEAD_DIM
    n_rg = bk // RED_ROWS

    def rg(r):
        return slice(r * RED_ROWS, (r + 1) * RED_ROWS)

    def chunk(kc):
        return pl.ds(pl.multiple_of(kc * bk, bk), bk)

    def for_range_by_two(n, step):
        def body(j, carry):
            step(2 * j, False)
            step(2 * j + 1, True)
            return carry
        lax.fori_loop(0, lax.shift_right_logical(n, 1), body, 0)

        @pl.when((n & 1) == 1)
        def _():
            step(n - 1, True)

    row_head = lax.broadcasted_iota(jnp.int32, (LANES, bq), 0) // HEAD_DIM
    for h in range(N_HEADS):
        slab = slice((h // pair) * LANES, (h // pair + 1) * LANES)
        keep = row_head == (h % pair)
        qp_ref[h] = jnp.where(keep, qt_ref[0, slab, :], jnp.zeros((), BF16))
        qip_ref[h] = jnp.where(keep, qit_ref[0, slab, :], jnp.zeros((), BF16))

    qpos = i * bq + lax.broadcasted_iota(jnp.int32, (RED_ROWS, bq), 1)
    krow = lax.broadcasted_iota(jnp.int32, (RED_ROWS, bq), 0)
    wit = wit_ref[0] * IDX_SCALE

    def idx_step(kc, may_be_last):
        for half in range(IDX_SPLIT):
            hrows = bk // IDX_SPLIT
            base = half * hrows
            kib = ki_ref[0, pl.ds(pl.multiple_of(kc * bk + base, hrows), hrows), :]
            acc = jnp.zeros((hrows, bq), F32)
            for h in range(IDX_HEADS):
                l = jnp.dot(kib, qip_ref[h], preferred_element_type=F32)
                acc = acc + jnp.maximum(l, 0.0) * wit[h:h + 1, :]
            for r in range(hrows // RED_ROWS):
                rows = slice(base + r * RED_ROWS, base + (r + 1) * RED_ROWS)
                sc = acc[rg(r)]
                if may_be_last:
                    kpos = kc * bk + base + r * RED_ROWS + krow
                    sc = jnp.where(kpos <= qpos, sc, -jnp.inf)
                sc_ref[kc, rows, :] = sc
                bits = lax.bitcast_convert_type(sc, jnp.int32)
                key = jnp.where(bits < 0, bits ^ np.int32(0x7FFFFFFF), bits)
                hi_ref[kc, rows, :] = (key >> 16).astype(jnp.int16)
                lo_ref[kc, rows, :] = ((key & 0xFFFF) - 32768).astype(jnp.int16)

    for_range_by_two(nk, idx_step)

    one16, zero16, min16 = (jnp.full((), v, jnp.int16) for v in (1, 0, -32768))

    def select_keys(n):
        def reduce_keys(ref, fn, init, combine):
            acc = init
            for kc in range(n):
                for r in range(n_rg):
                    acc = combine(acc, fn(ref[kc, rg(r), :], kc * bk + r * RED_ROWS))
            return acc

        def count16(ref, pred):
            acc = reduce_keys(ref, lambda x, off: jnp.where(pred(x), one16, zero16),
                              jnp.zeros((RED_ROWS, bq), jnp.int16), jnp.add)
            return jnp.sum(acc.astype(jnp.int32), axis=0, keepdims=True)

        def count(pred):
            acc = reduce_keys(sc_ref, lambda s, off: jnp.where(pred(s, off), 1.0, 0.0),
                              jnp.zeros((RED_ROWS, bq), F32), jnp.add)
            return jnp.sum(acc, axis=0, keepdims=True)

        def key_min(fn):
            acc = reduce_keys(sc_ref, lambda s, off: fn(s), jnp.full((RED_ROWS, bq), jnp.inf, F32), jnp.minimum)
            return jnp.min(acc, axis=0, keepdims=True)

        def bisect16(ref, base):
            def body(it, u):
                trial = u | jnp.left_shift(jnp.int32(1), 15 - it)
                cand = (trial - 32768).astype(jnp.int16)
                cnt = base + count16(ref, lambda x: x >= cand)
                return jnp.where(cnt >= topk, trial, u)
            return lax.fori_loop(0, 16, body, jnp.zeros((1, bq), jnp.int32))

        hi_star = bisect16(hi_ref, 0) - 32768
        hi16 = hi_star.astype(jnp.int16)
        above = jnp.zeros((RED_ROWS, bq), jnp.int16)
        for kc in range(n):
            for r in range(n_rg):
                hi = hi_ref[kc, rg(r), :]
                lo_ref[kc, rg(r), :] = jnp.where(hi == hi16, lo_ref[kc, rg(r), :], min16)
                above = above + jnp.where(hi > hi16, one16, zero16)
        above = jnp.sum(above.astype(jnp.int32), axis=0, keepdims=True)
        lo_star = bisect16(lo_ref, above)
        key = jnp.maximum(hi_star * 65536 + lo_star, KEY_NEG_F32_MAX)
        thr0 = lax.bitcast_convert_type(jnp.where(key < 0, key ^ np.int32(0x7FFFFFFF), key), F32)

        t = key_min(lambda s: jnp.where(s >= thr0, s, jnp.inf))
        c_gt = count(lambda s, off: s > t)

        def adv_body(st):
            t, c_gt, _ = st
            t_next = key_min(lambda s: jnp.where(s > t, s, jnp.inf))
            t = jnp.where(c_gt >= topk_f, t_next, t)
            c_gt = count(lambda s, off: s > t)
            return t, c_gt, jnp.max(c_gt)

        t, c_gt, _ = lax.while_loop(lambda st: st[2] >= topk_f, adv_body, (t, c_gt, jnp.max(c_gt)))

        c_ge = count(lambda s, off: s >= t)
        rem = topk_f - c_gt

        def tie_split():
            def body(it, p):
                trial = p | jnp.left_shift(jnp.int32(1), idx_bits - 1 - it)
                cnt = count(lambda s, off: (s == t) & (off + krow < trial))
                return jnp.where(cnt < rem, trial, p)
            return lax.fori_loop(0, idx_bits, body, jnp.zeros((1, bq), jnp.int32))

        def mask_pass(selected):
            for kc in range(n):
                for r in range(n_rg):
                    s = sc_ref[kc, rg(r), :]
                    sc_ref[kc, rg(r), :] = jnp.where(selected(s, kc * bk + r * RED_ROWS), 0.0, -jnp.inf)

        def mask_with_ties():
            last_eq = tie_split()
            mask_pass(lambda s, off: (s > t) | ((s == t) & (off + krow <= last_eq)))

        need_split = jnp.max(c_ge - c_gt - rem) > 0.0
        lax.cond(need_split, mask_with_ties, lambda: mask_pass(lambda s, off: s >= t))

    lax.switch(nk - 1, [functools.partial(select_keys, n) for n in range(1, max_chunks + 1)])

    m_ref[...] = jnp.full(m_ref.shape, -jnp.inf, F32)
    acc_ref[...] = jnp.zeros(acc_ref.shape, F32)

    def logits(kc, h):
        s = jnp.dot(k_ref[0, h // pair, chunk(kc), :], qp_ref[h], preferred_element_type=F32)
        mx = jnp.full((RED_ROWS, bq), -jnp.inf, F32)
        for r in range(n_rg):
            sb = s[rg(r)] + sc_ref[kc, rg(r), :]
            s_ref[h, rg(r), :] = sb
            mx = jnp.maximum(mx, sb)
        cm_ref[pl.ds(h, 1), :] = jnp.max(mx, axis=0, keepdims=True)

    def accumulate(kc, h):
        hrows = pl.ds(h * VT_ROWS, VT_ROWS)
        m_old = m_ref[pl.ds(h, 1), :]
        m_new = jnp.maximum(m_old, cm_ref[pl.ds(h, 1), :])
        m_use = jnp.where(m_new == -jnp.inf, 0.0, m_new)
        alpha = jnp.exp2(m_old - m_use)
        for r in range(n_rg):
            p_ref[h, rg(r), :] = jnp.exp2(s_ref[h, rg(r), :] - m_use).astype(BF16)
        pv = jnp.dot(vt_ref[0, kc, hrows, :], p_ref[h], preferred_element_type=F32)
        acc_ref[hrows, :] = alpha * acc_ref[hrows, :] + pv
        m_ref[pl.ds(h, 1), :] = m_new

    def for_heads(fn):
        for h in range(N_HEADS):
            fn(h)

    for_heads(lambda h: logits(jnp.int32(0), h))

    def att_step(kc):
        def both(h):
            accumulate(kc, h)
            logits(kc + 1, h)
        for_heads(both)

    def att_body(kc, carry):
        att_step(kc)
        return carry

    lax.fori_loop(0, nk - 1, att_body, 0)
    for_heads(lambda h: accumulate(nk - 1, h))

    outs = [acc_ref[h * VT_ROWS:h * VT_ROWS + HEAD_DIM, :] / acc_ref[h * VT_ROWS + HEAD_DIM:h * VT_ROWS + HEAD_DIM + 1, :]
            for h in range(N_HEADS)]
    o_ref[0] = jnp.concatenate(outs, axis=0).T.astype(BF16)


def _attention(qt, qit, wit, k, vt, ki):
    b, _, t = qt.shape
    bq, bk = ATT_BQ, ATT_BK
    topk = min(TOPK_MAX, t // 4)
    assert t % bk == 0 and t % bq == 0 and bq % LANES == 0 and bk % RED_ROWS == 0 and bk >= topk
    assert (bk // RED_ROWS) * (t // bk) < 2 ** 15
    idx_bits = max(1, int(np.ceil(np.log2(t))))
    qspec = lambda w: pl.BlockSpec((1, bq, w), lambda bi, i: (bi, i, 0))
    qtspec = lambda w: pl.BlockSpec((1, w, bq), lambda bi, i: (bi, 0, i))
    kspec = lambda w: pl.BlockSpec((1, t, w), lambda bi, i: (bi, 0, 0))
    return pl.pallas_call(
        functools.partial(_attn_kernel, bq=bq, bk=bk, topk=topk, idx_bits=idx_bits, max_chunks=t // bk),
        grid=(b, t // bq),
        in_specs=[
            qtspec(ATT_W), qtspec(IDX_W),
            pl.BlockSpec((1, IDX_HEADS, bq), lambda bi, i: (bi, 0, i)),
            pl.BlockSpec((1, ATT_W // LANES, t, LANES), lambda bi, i: (bi, 0, 0, 0)),
            pl.BlockSpec((1, t // bk, N_HEADS * VT_ROWS, bk), lambda bi, i: (bi, 0, 0, 0)),
            kspec(LANES),
        ],
        out_specs=qspec(ATT_W),
        out_shape=jax.ShapeDtypeStruct((b, t, ATT_W), BF16),
        scratch_shapes=[
            pltpu.VMEM((t // bk, bk, bq), F32),
            pltpu.VMEM((t // bk, bk, bq), jnp.int16),
            pltpu.VMEM((t // bk, bk, bq), jnp.int16),
            pltpu.VMEM((N_HEADS, LANES, bq), BF16),
            pltpu.VMEM((IDX_HEADS, LANES, bq), BF16),
            pltpu.VMEM((N_HEADS, bk, bq), F32),
            pltpu.VMEM((N_HEADS, bk, bq), BF16),
            pltpu.VMEM((N_HEADS, bq), F32),
            pltpu.VMEM((N_HEADS, bq), F32),
            pltpu.VMEM((N_HEADS * VT_ROWS, bq), F32),
        ],
        compiler_params=_cparams(2),
        name="dsa_attention",
    )(qt, qit, wit, k, vt, ki)


def _merge_kernel(x_ref, g_ref, wg_ref, bg_ref, ya_ref, yb_ref, wpa_ref, wpb_ref, wo_ref, o_ref):
    x = x_ref[...]
    d = x.shape[1]
    h = _rms(x, g_ref[...]).astype(BF16)
    gates = jax.nn.sigmoid(jnp.dot(h, wg_ref[...], preferred_element_type=F32) + bg_ref[...])
    ya = jnp.dot(ya_ref[...], wpa_ref[...], preferred_element_type=F32)
    yb = jnp.dot(yb_ref[...], wpb_ref[...], preferred_element_type=F32)
    m = (gates[:, :d] * ya + gates[:, d:] * yb).astype(BF16)
    o_ref[...] = x + jnp.dot(m, wo_ref[...], preferred_element_type=F32)


def _merge(x, g, w_g, b_gate, ya, yb, w_pa, w_pb, w_out):
    n, d = x.shape
    tm = ROW_TM
    assert n % tm == 0
    full = lambda i: (0, 0)
    row = lambda i: (i, 0)
    return pl.pallas_call(
        _merge_kernel,
        grid=(n // tm,),
        in_specs=[
            pl.BlockSpec((tm, d), row),
            pl.BlockSpec((1, d), full),
            pl.BlockSpec((d, 2 * d), full),
            pl.BlockSpec((1, 2 * d), full),
            pl.BlockSpec((tm, A_HALF), row),
            pl.BlockSpec((tm, ATT_W), row),
            pl.BlockSpec((A_HALF, d), full),
            pl.BlockSpec((ATT_W, d), full),
            pl.BlockSpec((d, d), full),
        ],
        out_specs=pl.BlockSpec((tm, d), row),
        out_shape=jax.ShapeDtypeStruct((n, d), F32),
        compiler_params=_cparams(1),
        name="merge",
    )(x, g.reshape(1, d), w_g, b_gate.reshape(1, 2 * d), ya, yb, w_pa, w_pb, w_out)


def _rope_tables(seq):
    inv = ROPE_THETA ** (-jnp.arange(0, HEAD_DIM, 2, dtype=F32) / HEAD_DIM)
    ang = jnp.arange(seq, dtype=jnp.int32).astype(F32)[:, None] * inv[None, :]
    cos, sin = jnp.cos(ang), jnp.sin(ang)
    reps = LANES // HEAD_DIM
    cos2 = jnp.tile(jnp.concatenate([cos, cos], axis=1), (1, reps))
    sin2 = jnp.tile(jnp.concatenate([-sin, sin], axis=1), (1, reps))
    return cos2, sin2


def kernel(x, ffn1_norm, ffn1_w_gu, ffn1_w_down, mix_norm, w_in, b_gate, gmlp_ln_g, gmlp_ln_b, gmlp_w_s, gmlp_b_s, w_pa, w_pb, w_out, ffn2_norm, ffn2_w_gu, ffn2_w_down, final_norm):
    b, t, d = x.shape
    depth = ffn1_norm.shape[0]
    n = b * t
    cos2, sin2 = _rope_tables(t)
    c_uv = 2 * A_HALF
    c_qkv = c_uv + 3 * ATT_W
    c_qi = c_qkv + IDX_W
    c_ki = c_qi + IDX_DIM
    c_wi = c_ki + IDX_HEADS
    xf = x.reshape(n, d)
    for l in range(depth):
        w_in_l = w_in[l].astype(BF16)
        w_ki2 = jnp.concatenate([w_in_l[:, c_qi:c_ki]] * (LANES // IDX_DIM), axis=1)
        w_wi = jnp.pad(w_in_l[:, c_ki:c_wi], ((0, 0), (0, LANES - IDX_HEADS)))
        last = l == depth - 1
        xf = _ffn(xf, ffn1_norm[l], ffn1_w_gu[l].astype(BF16), ffn1_w_down[l].astype(BF16))
        ya = _gmlp(xf, mix_norm[l], w_in_l[:, :c_uv], gmlp_ln_g[l], gmlp_ln_b[l], gmlp_w_s[l], gmlp_b_s[l])
        qt, k, vt, qit, ki, wit = _prep(xf, mix_norm[l], w_in_l[:, c_uv:c_qkv], w_in_l[:, c_qkv:c_qi],
                                      w_ki2, w_wi, cos2, sin2, b, t)
        r3 = lambda a: a.reshape(b, t, a.shape[-1])
        yb = _attention(qt, qit, wit, k, vt, r3(ki)).reshape(n, ATT_W)
        xf = _merge(xf, mix_norm[l], w_in_l[:, c_wi:], b_gate[l], ya, yb,
                    w_pa[l].astype(BF16), w_pb[l].astype(BF16), w_out[l].astype(BF16))
        xf = _ffn(xf, ffn2_norm[l], ffn2_w_gu[l].astype(BF16), ffn2_w_down[l].astype(BF16),
                  final_g=final_norm if last else None)
    return xf.reshape(b, t, d)
```

```python
import functools

import jax
import jax.numpy as jnp
import numpy as np
from jax import lax
from jax.experimental import pallas as pl
from jax.experimental.pallas import tpu as pltpu

F32 = jnp.float32
BF16 = jnp.bfloat16

CHUNK = 128
A_GROUPS = 4
A_GROUP_CH = 128
A_HALF = A_GROUPS * A_GROUP_CH
N_HEADS = 8
HEAD_DIM = 64
ATT_W = N_HEADS * HEAD_DIM
IDX_HEADS = 8
IDX_DIM = 64
IDX_W = IDX_HEADS * IDX_DIM
TOPK_MAX = 256
ROPE_THETA = 10000.0
EPS = 1e-6
IDX_SCALE = (IDX_DIM ** -0.5) * (IDX_HEADS ** -0.5)
ATT_SCALE = HEAD_DIM ** -0.5
LOG2_E = float(np.log2(np.e))

LANES = 128
SUBLANES = 8
VMEM_LIMIT_BYTES = 56 * 1024 * 1024

FFN_TM = 512
ROW_TM = 512
ATT_BQ = 256
ATT_BK = ROW_TM
RED_ROWS = 64
IDX_SPLIT = 8
VT_ROWS = 80

KEY_NEG_F32_MAX = np.int32(-2139095040)
F32_MAX = float(np.finfo(np.float32).max)


def _rms(x, g):
    return x * lax.rsqrt(jnp.mean(x * x, axis=-1, keepdims=True) + EPS) * g


def _cparams(n_axes):
    return pltpu.CompilerParams(
        dimension_semantics=("arbitrary",) * n_axes, vmem_limit_bytes=VMEM_LIMIT_BYTES)


def _ffn_kernel(x_ref, g_ref, wg_ref, wu_ref, wd_ref, *rest, final):
    if final:
        gf_ref, o_ref = rest
    else:
        (o_ref,) = rest
    x = x_ref[...]
    hn = _rms(x, g_ref[...]).astype(BF16)
    gate = jnp.dot(hn, wg_ref[...], preferred_element_type=F32)
    up = jnp.dot(hn, wu_ref[...], preferred_element_type=F32)
    a = (jax.nn.silu(gate) * up).astype(BF16)
    y = x + 0.5 * jnp.dot(a, wd_ref[...], preferred_element_type=F32)
    if final:
        y = _rms(y, gf_ref[...])
    o_ref[...] = y


def _ffn(x, g, w_gu, w_down, final_g=None):
    n, d = x.shape
    f = w_down.shape[0]
    tm = FFN_TM
    assert n % tm == 0 and f % LANES == 0
    final = final_g is not None
    resident = dict(pipeline_mode=pl.Buffered(1))
    in_specs = [
        pl.BlockSpec((tm, d), lambda i: (i, 0)),
        pl.BlockSpec((1, d), lambda i: (0, 0)),
        pl.BlockSpec((d, f), lambda i: (0, 0), **resident),
        pl.BlockSpec((d, f), lambda i: (0, 1), **resident),
        pl.BlockSpec((f, d), lambda i: (0, 0), **resident),
    ]
    args = [x, g.reshape(1, d), w_gu, w_gu, w_down]
    if final:
        in_specs.append(pl.BlockSpec((1, d), lambda i: (0, 0)))
        args.append(final_g.reshape(1, d))
    return pl.pallas_call(
        functools.partial(_ffn_kernel, final=final),
        grid=(n // tm,),
        in_specs=in_specs,
        out_specs=pl.BlockSpec((tm, d), lambda i: (i, 0)),
        out_shape=jax.ShapeDtypeStruct((n, d), F32),
        compiler_params=_cparams(1),
        name="ffn",
    )(*args)


def _gmlp_kernel(x_ref, g_ref, wuv_ref, lng_ref, lnb_ref, ws_ref, bs_ref, o_ref, *, tm):
    h = _rms(x_ref[...], g_ref[...]).astype(BF16)
    uv = jnp.dot(h, wuv_ref[...], preferred_element_type=F32)
    u = jax.nn.gelu(uv[:, :A_HALF])
    v = jax.nn.gelu(uv[:, A_HALF:])
    mu = jnp.mean(v, axis=-1, keepdims=True)
    var = jnp.mean(jnp.square(v - mu), axis=-1, keepdims=True)
    v = ((v - mu) * lax.rsqrt(var + EPS) * lng_ref[...] + lnb_ref[...]).astype(BF16)
    tri = (lax.broadcasted_iota(jnp.int32, (CHUNK, CHUNK), 1)
           <= lax.broadcasted_iota(jnp.int32, (CHUNK, CHUNK), 0))
    for g in range(A_GROUPS):
        ws = jnp.where(tri, ws_ref[g], 0.0).astype(BF16)
        cols = slice(g * A_GROUP_CH, (g + 1) * A_GROUP_CH)
        for c in range(tm // CHUNK):
            rows = slice(c * CHUNK, (c + 1) * CHUNK)
            mixed = jnp.dot(ws, v[rows, cols], preferred_element_type=F32) + bs_ref[g]
            o_ref[rows, cols] = (u[rows, cols] * mixed).astype(BF16)


def _gmlp(x, g, w_uv, ln_g, ln_b, w_s, b_s):
    n, d = x.shape
    tm = ROW_TM
    assert n % tm == 0 and tm % CHUNK == 0
    bs_b = jnp.broadcast_to(b_s[:, :, None], (A_GROUPS, CHUNK, A_GROUP_CH))
    return pl.pallas_call(
        functools.partial(_gmlp_kernel, tm=tm),
        grid=(n // tm,),
        in_specs=[
            pl.BlockSpec((tm, d), lambda i: (i, 0)),
            pl.BlockSpec((1, d), lambda i: (0, 0)),
            pl.BlockSpec((d, 2 * A_HALF), lambda i: (0, 0)),
            pl.BlockSpec((1, A_HALF), lambda i: (0, 0)),
            pl.BlockSpec((1, A_HALF), lambda i: (0, 0)),
            pl.BlockSpec((A_GROUPS, CHUNK, CHUNK), lambda i: (0, 0, 0)),
            pl.BlockSpec((A_GROUPS, CHUNK, A_GROUP_CH), lambda i: (0, 0, 0)),
        ],
        out_specs=pl.BlockSpec((tm, A_HALF), lambda i: (i, 0)),
        out_shape=jax.ShapeDtypeStruct((n, A_HALF), BF16),
        compiler_params=_cparams(1),
        name="gmlp",
    )(x, g.reshape(1, d), w_uv, ln_g.reshape(1, A_HALF), ln_b.reshape(1, A_HALF), w_s, bs_b)


def _rope(x, cos2, sin2):
    lane = lax.broadcasted_iota(jnp.int32, (x.shape[0], LANES), 1)
    first_half = (lane % HEAD_DIM) < (HEAD_DIM // 2)
    outs = []
    for c in range(x.shape[1] // LANES):
        xb = x[:, c * LANES:(c + 1) * LANES]
        partner = jnp.where(first_half,
                            pltpu.roll(xb, LANES - HEAD_DIM // 2, axis=1),
                            pltpu.roll(xb, HEAD_DIM // 2, axis=1))
        outs.append(xb * cos2 + partner * sin2)
    return outs[0] if len(outs) == 1 else jnp.concatenate(outs, axis=1)


def _prep_kernel(x_ref, g_ref, wqkv_ref, wqi_ref, wki_ref, wwi_ref, cos_ref, sin_ref,
                 qt_ref, k_ref, vt_ref, qit_ref, ki_ref, wit_ref):
    h = _rms(x_ref[...], g_ref[...]).astype(BF16)
    cos2, sin2 = cos_ref[...], sin_ref[...]
    qkv = jnp.dot(h, wqkv_ref[...], preferred_element_type=F32)
    qt_ref[0] = (_rope(qkv[:, :ATT_W], cos2, sin2) * (ATT_SCALE * LOG2_E)).T.astype(BF16)
    k = _rope(qkv[:, ATT_W:2 * ATT_W], cos2, sin2).astype(BF16)
    for c in range(ATT_W // LANES):
        k_ref[0, c] = k[:, c * LANES:(c + 1) * LANES]
    vt = qkv[:, 2 * ATT_W:].T
    tm = vt.shape[1]
    ones_rows = (lax.broadcasted_iota(jnp.int32, (VT_ROWS - HEAD_DIM, tm), 0) == 0).astype(F32)
    for hd in range(N_HEADS):
        vh = jnp.concatenate([vt[hd * HEAD_DIM:(hd + 1) * HEAD_DIM], ones_rows], axis=0)
        vt_ref[0, 0, hd * VT_ROWS:(hd + 1) * VT_ROWS, :] = vh.astype(BF16)
    qi = jnp.dot(h, wqi_ref[...], preferred_element_type=F32)
    qit_ref[0] = _rope(qi, cos2, sin2).T.astype(BF16)
    ki2 = jnp.dot(h, wki_ref[...], preferred_element_type=F32)
    ki_ref[...] = _rope(ki2, cos2, sin2).astype(BF16)
    wi = jnp.dot(h, wwi_ref[...], preferred_element_type=F32)
    wit_ref[0] = wi.T[:IDX_HEADS, :]


def _prep(x, g, w_qkv, w_qi, w_ki2, w_wi, cos2, sin2, batch, seq):
    n, d = x.shape
    tm = ROW_TM
    assert n % tm == 0 and seq % tm == 0 and tm == ATT_BK
    tpb = seq // tm
    full = lambda i: (0, 0)
    row = lambda i: (i, 0)
    pos = lambda i: (i % tpb, 0)
    return pl.pallas_call(
        _prep_kernel,
        grid=(n // tm,),
        in_specs=[
            pl.BlockSpec((tm, d), row),
            pl.BlockSpec((1, d), full),
            pl.BlockSpec((d, 3 * ATT_W), full),
            pl.BlockSpec((d, IDX_W), full),
            pl.BlockSpec((d, LANES), full),
            pl.BlockSpec((d, LANES), full),
            pl.BlockSpec((tm, LANES), pos),
            pl.BlockSpec((tm, LANES), pos),
        ],
        out_specs=[
            pl.BlockSpec((1, ATT_W, tm), lambda i: (i // tpb, 0, i % tpb)),
            pl.BlockSpec((1, ATT_W // LANES, tm, LANES), lambda i: (i // tpb, 0, i % tpb, 0)),
            pl.BlockSpec((1, 1, N_HEADS * VT_ROWS, tm), lambda i: (i // tpb, i % tpb, 0, 0)),
            pl.BlockSpec((1, IDX_W, tm), lambda i: (i // tpb, 0, i % tpb)),
            pl.BlockSpec((tm, LANES), row),
            pl.BlockSpec((1, IDX_HEADS, tm), lambda i: (i // tpb, 0, i % tpb)),
        ],
        out_shape=[
            jax.ShapeDtypeStruct((batch, ATT_W, seq), BF16),
            jax.ShapeDtypeStruct((batch, ATT_W // LANES, seq, LANES), BF16),
            jax.ShapeDtypeStruct((batch, tpb, N_HEADS * VT_ROWS, tm), BF16),
            jax.ShapeDtypeStruct((batch, IDX_W, seq), BF16),
            jax.ShapeDtypeStruct((n, LANES), BF16),
            jax.ShapeDtypeStruct((batch, IDX_HEADS, seq), F32),
        ],
        compiler_params=_cparams(1),
        name="attn_prep",
    )(x, g.reshape(1, d), w_qkv, w_qi, w_ki2, w_wi, cos2, sin2)


def _attn_kernel(qt_ref, qit_ref, wit_ref, k_ref, vt_ref, ki_ref, o_ref,
                 sc_ref, hi_ref, lo_ref, top_ref, qp_ref, qip_ref, s_ref, p_ref, cm_ref, m_ref, acc_ref,
                 *, bq, bk, topk, idx_bits, max_chunks):
    i = pl.program_id(1)
    nk = (i * bq + bq + bk - 1) // bk
    topk_f = float(topk)
    pair = LANES // HEAD_DIM
    n_rg = bk // RED_ROWS

    def rg(r):
        return slice(r * RED_ROWS, (r + 1) * RED_ROWS)

    def chunk(kc):
        return pl.ds(pl.multiple_of(kc * bk, bk), bk)

    def for_range_by_two(n, step):
        def body(j, carry):
            step(2 * j, False)
            step(2 * j + 1, True)
            return carry
        lax.fori_loop(0, lax.shift_right_logical(n, 1), body, 0)

        @pl.when((n & 1) == 1)
        def _():
            step(n - 1, True)

    row_head = lax.broadcasted_iota(jnp.int32, (LANES, bq), 0) // HEAD_DIM
    for h in range(N_HEADS):
        slab = slice((h // pair) * LANES, (h // pair + 1) * LANES)
        keep = row_head == (h % pair)
        qp_ref[h] = jnp.where(keep, qt_ref[0, slab, :], jnp.zeros((), BF16))
        qip_ref[h] = jnp.where(keep, qit_ref[0, slab, :], jnp.zeros((), BF16))

    qpos = i * bq + lax.broadcasted_iota(jnp.int32, (RED_ROWS, bq), 1)
    krow = lax.broadcasted_iota(jnp.int32, (RED_ROWS, bq), 0)
    wit = wit_ref[0] * IDX_SCALE

    def idx_step(kc, may_be_last):
        for half in range(IDX_SPLIT):
            hrows = bk // IDX_SPLIT
            base = half * hrows
            kib = ki_ref[0, pl.ds(pl.multiple_of(kc * bk + base, hrows), hrows), :]
            acc = jnp.zeros((hrows, bq), F32)
            for h in range(IDX_HEADS):
                l = jnp.dot(kib, qip_ref[h], preferred_element_type=F32)
                acc = acc + jnp.maximum(l, 0.0) * wit[h:h + 1, :]
            for r in range(hrows // RED_ROWS):
                rows = slice(base + r * RED_ROWS, base + (r + 1) * RED_ROWS)
                sc = acc[rg(r)]
                if may_be_last:
                    kpos = kc * bk + base + r * RED_ROWS + krow
                    sc = jnp.where(kpos <= qpos, sc, -jnp.inf)
                sc_ref[kc, rows, :] = sc
                bits = lax.bitcast_convert_type(sc, jnp.int32)
                key = jnp.where(bits < 0, bits ^ np.int32(0x7FFFFFFF), bits)
                hi_ref[kc, rows, :] = (key >> 16).astype(jnp.int16)
                lo_ref[kc, rows, :] = ((key & 0xFFFF) - 32768).astype(jnp.int16)

    for_range_by_two(nk, idx_step)

    one16, zero16, min16 = (jnp.full((), v, jnp.int16) for v in (1, 0, -32768))

    def select_keys(n):
        def reduce_keys(ref, fn, init, combine):
            acc = init
            for kc in range(n):
                for r in range(n_rg):
                    acc = combine(acc, fn(ref[kc, rg(r), :], kc * bk + r * RED_ROWS))
            return acc

        def total16(acc):
            return jnp.sum(acc.astype(jnp.int32), axis=0, keepdims=True)

        def count16(ref, pred):
            return total16(reduce_keys(ref, lambda x, off: jnp.where(pred(x), one16, zero16),
                                       jnp.zeros((RED_ROWS, bq), jnp.int16), jnp.add))

        def count16_top(pred):
            acc = jnp.zeros((RED_ROWS, bq), jnp.int16)
            for j in range(2 * n):
                acc = acc + jnp.where(pred(top_ref[j]), one16, zero16)
            return total16(acc)

        def count(pred):
            acc = reduce_keys(sc_ref, lambda s, off: jnp.where(pred(s, off), 1.0, 0.0),
                              jnp.zeros((RED_ROWS, bq), F32), jnp.add)
            return jnp.sum(acc, axis=0, keepdims=True)

        def key_min(fn):
            acc = reduce_keys(sc_ref, lambda s, off: fn(s), jnp.full((RED_ROWS, bq), jnp.inf, F32), jnp.minimum)
            return jnp.min(acc, axis=0, keepdims=True)

        def bisect16(count_ge, base):
            def body(it, u):
                trial = u | jnp.left_shift(jnp.int32(1), 15 - it)
                cand = (trial - 32768).astype(jnp.int16)
                return jnp.where(base + count_ge(cand) >= topk, trial, u)
            return lax.fori_loop(0, 16, body, jnp.zeros((1, bq), jnp.int32))

        hi_star = bisect16(lambda c: count16(hi_ref, lambda x: x >= c), 0) - 32768
        hi16 = hi_star.astype(jnp.int16)
        above = jnp.zeros((RED_ROWS, bq), jnp.int16)
        n_cand = jnp.zeros((RED_ROWS, bq), jnp.int16)
        for kc in range(n):
            top1 = jnp.full((RED_ROWS, bq), min16, jnp.int16)
            top2 = jnp.full((RED_ROWS, bq), min16, jnp.int16)
            for r in range(n_rg):
                hi = hi_ref[kc, rg(r), :]
                x = jnp.where(hi == hi16, lo_ref[kc, rg(r), :], min16)
                lo_ref[kc, rg(r), :] = x
                above = above + jnp.where(hi > hi16, one16, zero16)
                n_cand = n_cand + jnp.where(x > min16, one16, zero16)
                beats = x > top1
                lower = jnp.where(beats, top1, x)
                top1 = jnp.where(beats, x, top1)
                top2 = jnp.where(lower > top2, lower, top2)
            top_ref[2 * kc] = top1
            top_ref[2 * kc + 1] = top2
        above = total16(above)
        all_kept = jnp.min((total16(n_cand) == count16_top(lambda x: x > min16)).astype(jnp.int32)) == 1
        lo_star = lax.cond(all_kept,
                           lambda: bisect16(lambda c: count16_top(lambda x: x >= c), above),
                           lambda: bisect16(lambda c: count16(lo_ref, lambda x: x >= c), above))
        key = jnp.maximum(hi_star * 65536 + lo_star, KEY_NEG_F32_MAX)
        thr0 = lax.bitcast_convert_type(jnp.where(key < 0, key ^ np.int32(0x7FFFFFFF), key), F32)

        t = key_min(lambda s: jnp.where(s >= thr0, s, jnp.inf))
        c_gt = count(lambda s, off: s > t)

        def adv_body(st):
            t, c_gt, _ = st
            t_next = key_min(lambda s: jnp.where(s > t, s, jnp.inf))
            t = jnp.where(c_gt >= topk_f, t_next, t)
            c_gt = count(lambda s, off: s > t)
            return t, c_gt, jnp.max(c_gt)

        t, c_gt, _ = lax.while_loop(lambda st: st[2] >= topk_f, adv_body, (t, c_gt, jnp.max(c_gt)))

        c_ge = count(lambda s, off: s >= t)
        rem = topk_f - c_gt

        def tie_split():
            def body(it, p):
                trial = p | jnp.left_shift(jnp.int32(1), idx_bits - 1 - it)
                cnt = count(lambda s, off: (s == t) & (off + krow < trial))
                return jnp.where(cnt < rem, trial, p)
            return lax.fori_loop(0, idx_bits, body, jnp.zeros((1, bq), jnp.int32))

        def mask_pass(selected):
            for kc in range(n):
                for r in range(n_rg):
                    s = sc_ref[kc, rg(r), :]
                    sc_ref[kc, rg(r), :] = jnp.where(selected(s, kc * bk + r * RED_ROWS), 0.0, -jnp.inf)

        def mask_with_ties():
            last_eq = tie_split()
            mask_pass(lambda s, off: (s > t) | ((s == t) & (off + krow <= last_eq)))

        need_split = jnp.max(c_ge - c_gt - rem) > 0.0
        lax.cond(need_split, mask_with_ties, lambda: mask_pass(lambda s, off: s >= t))

    lax.switch(nk - 1, [functools.partial(select_keys, n) for n in range(1, max_chunks + 1)])

    m_ref[...] = jnp.full(m_ref.shape, -jnp.inf, F32)
    acc_ref[...] = jnp.zeros(acc_ref.shape, F32)

    def logits(kc, h):
        s = jnp.dot(k_ref[0, h // pair, chunk(kc), :], qp_ref[h], preferred_element_type=F32)
        mx = jnp.full((RED_ROWS, bq), -jnp.inf, F32)
        for r in range(n_rg):
            sb = s[rg(r)] + sc_ref[kc, rg(r), :]
            s_ref[h, rg(r), :] = sb
            mx = jnp.maximum(mx, sb)
        cm_ref[pl.ds(h, 1), :] = jnp.max(mx, axis=0, keepdims=True)

    def accumulate(kc, h):
        hrows = pl.ds(h * VT_ROWS, VT_ROWS)
        m_old = m_ref[pl.ds(h, 1), :]
        m_new = jnp.maximum(m_old, cm_ref[pl.ds(h, 1), :])
        m_use = jnp.where(m_new == -jnp.inf, 0.0, m_new)
        alpha = jnp.exp2(m_old - m_use)
        for r in range(n_rg):
            p_ref[h, rg(r), :] = jnp.exp2(s_ref[h, rg(r), :] - m_use).astype(BF16)
        pv = jnp.dot(vt_ref[0, kc, hrows, :], p_ref[h], preferred_element_type=F32)
        acc_ref[hrows, :] = alpha * acc_ref[hrows, :] + pv
        m_ref[pl.ds(h, 1), :] = m_new

    def for_heads(fn):
        for h in range(N_HEADS):
            fn(h)

    for_heads(lambda h: logits(jnp.int32(0), h))

    def att_step(kc):
        def both(h):
            accumulate(kc, h)
            logits(kc + 1, h)
        for_heads(both)

    def att_body(kc, carry):
        att_step(kc)
        return carry

    lax.fori_loop(0, nk - 1, att_body, 0)
    for_heads(lambda h: accumulate(nk - 1, h))

    outs = [acc_ref[h * VT_ROWS:h * VT_ROWS + HEAD_DIM, :] / acc_ref[h * VT_ROWS + HEAD_DIM:h * VT_ROWS + HEAD_DIM + 1, :]
            for h in range(N_HEADS)]
    o_ref[0] = jnp.concatenate(outs, axis=0).T.astype(BF16)


def _attention(qt, qit, wit, k, vt, ki):
    b, _, t = qt.shape
    bq, bk = ATT_BQ, ATT_BK
    topk = min(TOPK_MAX, t // 4)
    assert t % bk == 0 and t % bq == 0 and bq % LANES == 0 and bk % RED_ROWS == 0 and bk >= topk
    assert (bk // RED_ROWS) * (t // bk) < 2 ** 15
    idx_bits = max(1, int(np.ceil(np.log2(t))))
    qspec = lambda w: pl.BlockSpec((1, bq, w), lambda bi, i: (bi, i, 0))
    qtspec = lambda w: pl.BlockSpec((1, w, bq), lambda bi, i: (bi, 0, i))
    kspec = lambda w: pl.BlockSpec((1, t, w), lambda bi, i: (bi, 0, 0))
    return pl.pallas_call(
        functools.partial(_attn_kernel, bq=bq, bk=bk, topk=topk, idx_bits=idx_bits, max_chunks=t // bk),
        grid=(b, t // bq),
        in_specs=[
            qtspec(ATT_W), qtspec(IDX_W),
            pl.BlockSpec((1, IDX_HEADS, bq), lambda bi, i: (bi, 0, i)),
            pl.BlockSpec((1, ATT_W // LANES, t, LANES), lambda bi, i: (bi, 0, 0, 0)),
            pl.BlockSpec((1, t // bk, N_HEADS * VT_ROWS, bk), lambda bi, i: (bi, 0, 0, 0)),
            kspec(LANES),
        ],
        out_specs=qspec(ATT_W),
        out_shape=jax.ShapeDtypeStruct((b, t, ATT_W), BF16),
        scratch_shapes=[
            pltpu.VMEM((t // bk, bk, bq), F32),
            pltpu.VMEM((t // bk, bk, bq), jnp.int16),
            pltpu.VMEM((t // bk, bk, bq), jnp.int16),
            pltpu.VMEM((2 * (t // bk), RED_ROWS, bq), jnp.int16),
            pltpu.VMEM((N_HEADS, LANES, bq), BF16),
            pltpu.VMEM((IDX_HEADS, LANES, bq), BF16),
            pltpu.VMEM((N_HEADS, bk, bq), F32),
            pltpu.VMEM((N_HEADS, bk, bq), BF16),
            pltpu.VMEM((N_HEADS, bq), F32),
            pltpu.VMEM((N_HEADS, bq), F32),
            pltpu.VMEM((N_HEADS * VT_ROWS, bq), F32),
        ],
        compiler_params=_cparams(2),
        name="dsa_attention",
    )(qt, qit, wit, k, vt, ki)


def _merge_kernel(x_ref, g_ref, wg_ref, bg_ref, ya_ref, yb_ref, wpa_ref, wpb_ref, wo_ref, o_ref):
    x = x_ref[...]
    d = x.shape[1]
    h = _rms(x, g_ref[...]).astype(BF16)
    gates = jax.nn.sigmoid(jnp.dot(h, wg_ref[...], preferred_element_type=F32) + bg_ref[...])
    ya = jnp.dot(ya_ref[...], wpa_ref[...], preferred_element_type=F32)
    yb = jnp.dot(yb_ref[...], wpb_ref[...], preferred_element_type=F32)
    m = (gates[:, :d] * ya + gates[:, d:] * yb).astype(BF16)
    o_ref[...] = x + jnp.dot(m, wo_ref[...], preferred_element_type=F32)


def _merge(x, g, w_g, b_gate, ya, yb, w_pa, w_pb, w_out):
    n, d = x.shape
    tm = ROW_TM
    assert n % tm == 0
    full = lambda i: (0, 0)
    row = lambda i: (i, 0)
    return pl.pallas_call(
        _merge_kernel,
        grid=(n // tm,),
        in_specs=[
            pl.BlockSpec((tm, d), row),
            pl.BlockSpec((1, d), full),
            pl.BlockSpec((d, 2 * d), full),
            pl.BlockSpec((1, 2 * d), full),
            pl.BlockSpec((tm, A_HALF), row),
            pl.BlockSpec((tm, ATT_W), row),
            pl.BlockSpec((A_HALF, d), full),
            pl.BlockSpec((ATT_W, d), full),
            pl.BlockSpec((d, d), full),
        ],
        out_specs=pl.BlockSpec((tm, d), row),
        out_shape=jax.ShapeDtypeStruct((n, d), F32),
        compiler_params=_cparams(1),
        name="merge",
    )(x, g.reshape(1, d), w_g, b_gate.reshape(1, 2 * d), ya, yb, w_pa, w_pb, w_out)


def _rope_tables(seq):
    inv = ROPE_THETA ** (-jnp.arange(0, HEAD_DIM, 2, dtype=F32) / HEAD_DIM)
    ang = jnp.arange(seq, dtype=jnp.int32).astype(F32)[:, None] * inv[None, :]
    cos, sin = jnp.cos(ang), jnp.sin(ang)
    reps = LANES // HEAD_DIM
    cos2 = jnp.tile(jnp.concatenate([cos, cos], axis=1), (1, reps))
    sin2 = jnp.tile(jnp.concatenate([-sin, sin], axis=1), (1, reps))
    return cos2, sin2


def kernel(x, ffn1_norm, ffn1_w_gu, ffn1_w_down, mix_norm, w_in, b_gate, gmlp_ln_g, gmlp_ln_b, gmlp_w_s, gmlp_b_s, w_pa, w_pb, w_out, ffn2_norm, ffn2_w_gu, ffn2_w_down, final_norm):
    b, t, d = x.shape
    depth = ffn1_norm.shape[0]
    n = b * t
    cos2, sin2 = _rope_tables(t)
    c_uv = 2 * A_HALF
    c_qkv = c_uv + 3 * ATT_W
    c_qi = c_qkv + IDX_W
    c_ki = c_qi + IDX_DIM
    c_wi = c_ki + IDX_HEADS
    xf = x.reshape(n, d)
    for l in range(depth):
        w_in_l = w_in[l].astype(BF16)
        w_ki2 = jnp.concatenate([w_in_l[:, c_qi:c_ki]] * (LANES // IDX_DIM), axis=1)
        w_wi = jnp.pad(w_in_l[:, c_ki:c_wi], ((0, 0), (0, LANES - IDX_HEADS)))
        last = l == depth - 1
        xf = _ffn(xf, ffn1_norm[l], ffn1_w_gu[l].astype(BF16), ffn1_w_down[l].astype(BF16))
        ya = _gmlp(xf, mix_norm[l], w_in_l[:, :c_uv], gmlp_ln_g[l], gmlp_ln_b[l], gmlp_w_s[l], gmlp_b_s[l])
        qt, k, vt, qit, ki, wit = _prep(xf, mix_norm[l], w_in_l[:, c_uv:c_qkv], w_in_l[:, c_qkv:c_qi],
                                      w_ki2, w_wi, cos2, sin2, b, t)
        r3 = lambda a: a.reshape(b, t, a.shape[-1])
        yb = _attention(qt, qit, wit, k, vt, r3(ki)).reshape(n, ATT_W)
        xf = _merge(xf, mix_norm[l], w_in_l[:, c_wi:], b_gate[l], ya, yb,
                    w_pa[l].astype(BF16), w_pb[l].astype(BF16), w_out[l].astype(BF16))
        xf = _ffn(xf, ffn2_norm[l], ffn2_w_gu[l].astype(BF16), ffn2_w_down[l].astype(BF16),
                  final_g=final_norm if last else None)
    return xf.reshape(b, t, d)
```

```python
import functools

import jax
import jax.numpy as jnp
import numpy as np
from jax import lax
from jax.experimental import pallas as pl
from jax.experimental.pallas import tpu as pltpu

F32 = jnp.float32
BF16 = jnp.bfloat16

CHUNK = 128
A_GROUPS = 4
A_GROUP_CH = 128
A_HALF = A_GROUPS * A_GROUP_CH
N_HEADS = 8
HEAD_DIM = 64
ATT_W = N_HEADS * HEAD_DIM
IDX_HEADS = 8
IDX_DIM = 64
IDX_W = IDX_HEADS * IDX_DIM
TOPK_MAX = 256
ROPE_THETA = 10000.0
EPS = 1e-6
IDX_SCALE = (IDX_DIM ** -0.5) * (IDX_HEADS ** -0.5)
ATT_SCALE = HEAD_DIM ** -0.5
LOG2_E = float(np.log2(np.e))

LANES = 128
VMEM_LIMIT_BYTES = 56 * 1024 * 1024

FFN_TM = 512
ROW_TM = 512
ATT_BQ = 256
ATT_BK = ROW_TM
RED_ROWS = 64
IDX_SPLIT = 8
VT_ROWS = 80

KEY_NEG_F32_MAX = np.int32(-2139095040)


def _rms(x, g):
    return x * lax.rsqrt(jnp.mean(x * x, axis=-1, keepdims=True) + EPS) * g


def _cparams(n_axes):
    return pltpu.CompilerParams(
        dimension_semantics=("arbitrary",) * n_axes, vmem_limit_bytes=VMEM_LIMIT_BYTES)


def _ffn_kernel(x_ref, g_ref, wg_ref, wu_ref, wd_ref, *rest, final):
    if final:
        gf_ref, o_ref = rest
    else:
        (o_ref,) = rest
    x = x_ref[...]
    hn = _rms(x, g_ref[...]).astype(BF16)
    gate = jnp.dot(hn, wg_ref[...], preferred_element_type=F32)
    up = jnp.dot(hn, wu_ref[...], preferred_element_type=F32)
    a = (jax.nn.silu(gate) * up).astype(BF16)
    y = x + 0.5 * jnp.dot(a, wd_ref[...], preferred_element_type=F32)
    if final:
        y = _rms(y, gf_ref[...])
    o_ref[...] = y


def _ffn(x, g, w_gu, w_down, final_g=None):
    n, d = x.shape
    f = w_down.shape[0]
    tm = FFN_TM
    assert n % tm == 0 and f % LANES == 0
    final = final_g is not None
    resident = dict(pipeline_mode=pl.Buffered(1))
    in_specs = [
        pl.BlockSpec((tm, d), lambda i: (i, 0)),
        pl.BlockSpec((1, d), lambda i: (0, 0)),
        pl.BlockSpec((d, f), lambda i: (0, 0), **resident),
        pl.BlockSpec((d, f), lambda i: (0, 1), **resident),
        pl.BlockSpec((f, d), lambda i: (0, 0), **resident),
    ]
    args = [x, g.reshape(1, d), w_gu, w_gu, w_down]
    if final:
        in_specs.append(pl.BlockSpec((1, d), lambda i: (0, 0)))
        args.append(final_g.reshape(1, d))
    return pl.pallas_call(
        functools.partial(_ffn_kernel, final=final),
        grid=(n // tm,),
        in_specs=in_specs,
        out_specs=pl.BlockSpec((tm, d), lambda i: (i, 0)),
        out_shape=jax.ShapeDtypeStruct((n, d), F32),
        compiler_params=_cparams(1),
        name="ffn",
    )(*args)


def _gmlp_kernel(x_ref, g_ref, wuv_ref, lng_ref, lnb_ref, ws_ref, bs_ref, o_ref, *, tm):
    h = _rms(x_ref[...], g_ref[...]).astype(BF16)
    uv = jnp.dot(h, wuv_ref[...], preferred_element_type=F32)
    u = jax.nn.gelu(uv[:, :A_HALF])
    v = jax.nn.gelu(uv[:, A_HALF:])
    mu = jnp.mean(v, axis=-1, keepdims=True)
    var = jnp.mean(jnp.square(v - mu), axis=-1, keepdims=True)
    v = ((v - mu) * lax.rsqrt(var + EPS) * lng_ref[...] + lnb_ref[...]).astype(BF16)
    tri = (lax.broadcasted_iota(jnp.int32, (CHUNK, CHUNK), 1)
           <= lax.broadcasted_iota(jnp.int32, (CHUNK, CHUNK), 0))
    for g in range(A_GROUPS):
        ws = jnp.where(tri, ws_ref[g], 0.0).astype(BF16)
        cols = slice(g * A_GROUP_CH, (g + 1) * A_GROUP_CH)
        for c in range(tm // CHUNK):
            rows = slice(c * CHUNK, (c + 1) * CHUNK)
            mixed = jnp.dot(ws, v[rows, cols], preferred_element_type=F32) + bs_ref[g]
            o_ref[rows, cols] = (u[rows, cols] * mixed).astype(BF16)


def _gmlp(x, g, w_uv, ln_g, ln_b, w_s, b_s):
    n, d = x.shape
    tm = ROW_TM
    assert n % tm == 0 and tm % CHUNK == 0
    bs_b = jnp.broadcast_to(b_s[:, :, None], (A_GROUPS, CHUNK, A_GROUP_CH))
    return pl.pallas_call(
        functools.partial(_gmlp_kernel, tm=tm),
        grid=(n // tm,),
        in_specs=[
            pl.BlockSpec((tm, d), lambda i: (i, 0)),
            pl.BlockSpec((1, d), lambda i: (0, 0)),
            pl.BlockSpec((d, 2 * A_HALF), lambda i: (0, 0)),
            pl.BlockSpec((1, A_HALF), lambda i: (0, 0)),
            pl.BlockSpec((1, A_HALF), lambda i: (0, 0)),
            pl.BlockSpec((A_GROUPS, CHUNK, CHUNK), lambda i: (0, 0, 0)),
            pl.BlockSpec((A_GROUPS, CHUNK, A_GROUP_CH), lambda i: (0, 0, 0)),
        ],
        out_specs=pl.BlockSpec((tm, A_HALF), lambda i: (i, 0)),
        out_shape=jax.ShapeDtypeStruct((n, A_HALF), BF16),
        compiler_params=_cparams(1),
        name="gmlp",
    )(x, g.reshape(1, d), w_uv, ln_g.reshape(1, A_HALF), ln_b.reshape(1, A_HALF), w_s, bs_b)


def _rope(x, cos2, sin2):
    lane = lax.broadcasted_iota(jnp.int32, (x.shape[0], LANES), 1)
    first_half = (lane % HEAD_DIM) < (HEAD_DIM // 2)
    outs = []
    for c in range(x.shape[1] // LANES):
        xb = x[:, c * LANES:(c + 1) * LANES]
        partner = jnp.where(first_half,
                            pltpu.roll(xb, LANES - HEAD_DIM // 2, axis=1),
                            pltpu.roll(xb, HEAD_DIM // 2, axis=1))
        outs.append(xb * cos2 + partner * sin2)
    return outs[0] if len(outs) == 1 else jnp.concatenate(outs, axis=1)


def _prep_kernel(x_ref, g_ref, wqkv_ref, wqi_ref, wki_ref, wwi_ref, cos_ref, sin_ref,
                 qt_ref, k_ref, vt_ref, qit_ref, ki_ref, wit_ref):
    h = _rms(x_ref[...], g_ref[...]).astype(BF16)
    cos2, sin2 = cos_ref[...], sin_ref[...]
    qkv = jnp.dot(h, wqkv_ref[...], preferred_element_type=F32)
    qt_ref[0] = (_rope(qkv[:, :ATT_W], cos2, sin2) * (ATT_SCALE * LOG2_E)).T.astype(BF16)
    k = _rope(qkv[:, ATT_W:2 * ATT_W], cos2, sin2).astype(BF16)
    for c in range(ATT_W // LANES):
        k_ref[0, c] = k[:, c * LANES:(c + 1) * LANES]
    vt = qkv[:, 2 * ATT_W:].T
    tm = vt.shape[1]
    ones_rows = (lax.broadcasted_iota(jnp.int32, (VT_ROWS - HEAD_DIM, tm), 0) == 0).astype(F32)
    for hd in range(N_HEADS):
        vh = jnp.concatenate([vt[hd * HEAD_DIM:(hd + 1) * HEAD_DIM], ones_rows], axis=0)
        vt_ref[0, 0, hd * VT_ROWS:(hd + 1) * VT_ROWS, :] = vh.astype(BF16)
    qi = jnp.dot(h, wqi_ref[...], preferred_element_type=F32)
    qit_ref[0] = _rope(qi, cos2, sin2).T.astype(BF16)
    ki2 = jnp.dot(h, wki_ref[...], preferred_element_type=F32)
    ki_ref[...] = _rope(ki2, cos2, sin2).astype(BF16)
    wi = jnp.dot(h, wwi_ref[...], preferred_element_type=F32)
    wit_ref[0] = wi.T[:IDX_HEADS, :]


def _prep(x, g, w_qkv, w_qi, w_ki2, w_wi, cos2, sin2, batch, seq):
    n, d = x.shape
    tm = ROW_TM
    assert n % tm == 0 and seq % tm == 0 and tm == ATT_BK
    tpb = seq // tm
    full = lambda i: (0, 0)
    row = lambda i: (i, 0)
    pos = lambda i: (i % tpb, 0)
    return pl.pallas_call(
        _prep_kernel,
        grid=(n // tm,),
        in_specs=[
            pl.BlockSpec((tm, d), row),
            pl.BlockSpec((1, d), full),
            pl.BlockSpec((d, 3 * ATT_W), full),
            pl.BlockSpec((d, IDX_W), full),
            pl.BlockSpec((d, LANES), full),
            pl.BlockSpec((d, LANES), full),
            pl.BlockSpec((tm, LANES), pos),
            pl.BlockSpec((tm, LANES), pos),
        ],
        out_specs=[
            pl.BlockSpec((1, ATT_W, tm), lambda i: (i // tpb, 0, i % tpb)),
            pl.BlockSpec((1, ATT_W // LANES, tm, LANES), lambda i: (i // tpb, 0, i % tpb, 0)),
            pl.BlockSpec((1, 1, N_HEADS * VT_ROWS, tm), lambda i: (i // tpb, i % tpb, 0, 0)),
            pl.BlockSpec((1, IDX_W, tm), lambda i: (i // tpb, 0, i % tpb)),
            pl.BlockSpec((tm, LANES), row),
            pl.BlockSpec((1, IDX_HEADS, tm), lambda i: (i // tpb, 0, i % tpb)),
        ],
        out_shape=[
            jax.ShapeDtypeStruct((batch, ATT_W, seq), BF16),
            jax.ShapeDtypeStruct((batch, ATT_W // LANES, seq, LANES), BF16),
            jax.ShapeDtypeStruct((batch, tpb, N_HEADS * VT_ROWS, tm), BF16),
            jax.ShapeDtypeStruct((batch, IDX_W, seq), BF16),
            jax.ShapeDtypeStruct((n, LANES), BF16),
            jax.ShapeDtypeStruct((batch, IDX_HEADS, seq), F32),
        ],
        compiler_params=_cparams(1),
        name="attn_prep",
    )(x, g.reshape(1, d), w_qkv, w_qi, w_ki2, w_wi, cos2, sin2)


def _attn_kernel(qt_ref, qit_ref, wit_ref, k_ref, vt_ref, ki_ref, o_ref,
                 sc_ref, hi_ref, lo_ref, top_ref, qp_ref, qip_ref, s_ref, p_ref, cm_ref, m_ref, acc_ref,
                 *, bq, bk, topk, idx_bits, max_chunks):
    i = pl.program_id(1)
    nk = (i * bq + bq + bk - 1) // bk
    topk_f = float(topk)
    pair = LANES // HEAD_DIM
    n_rg = bk // RED_ROWS

    def rg(r):
        return slice(r * RED_ROWS, (r + 1) * RED_ROWS)

    def chunk(kc):
        return pl.ds(pl.multiple_of(kc * bk, bk), bk)

    def for_range_by_two(n, step):
        def body(j, carry):
            step(2 * j, False)
            step(2 * j + 1, True)
            return carry
        lax.fori_loop(0, lax.shift_right_logical(n, 1), body, 0)

        @pl.when((n & 1) == 1)
        def _():
            step(n - 1, True)

    row_head = lax.broadcasted_iota(jnp.int32, (LANES, bq), 0) // HEAD_DIM
    for h in range(N_HEADS):
        slab = slice((h // pair) * LANES, (h // pair + 1) * LANES)
        keep = row_head == (h % pair)
        qp_ref[h] = jnp.where(keep, qt_ref[0, slab, :], jnp.zeros((), BF16))
        qip_ref[h] = jnp.where(keep, qit_ref[0, slab, :], jnp.zeros((), BF16))

    qpos = i * bq + lax.broadcasted_iota(jnp.int32, (RED_ROWS, bq), 1)
    krow = lax.broadcasted_iota(jnp.int32, (RED_ROWS, bq), 0)
    wit = wit_ref[0] * IDX_SCALE

    def idx_step(kc, may_be_last):
        for half in range(IDX_SPLIT):
            hrows = bk // IDX_SPLIT
            base = half * hrows
            kib = ki_ref[0, pl.ds(pl.multiple_of(kc * bk + base, hrows), hrows), :]
            acc = jnp.zeros((hrows, bq), F32)
            for h in range(IDX_HEADS):
                l = jnp.dot(kib, qip_ref[h], preferred_element_type=F32)
                acc = acc + jnp.maximum(l, 0.0) * wit[h:h + 1, :]
            for r in range(hrows // RED_ROWS):
                rows = slice(base + r * RED_ROWS, base + (r + 1) * RED_ROWS)
                sc = acc[rg(r)]
                if may_be_last:
                    kpos = kc * bk + base + r * RED_ROWS + krow
                    sc = jnp.where(kpos <= qpos, sc, -jnp.inf)
                sc_ref[kc, rows, :] = sc
                bits = lax.bitcast_convert_type(sc, jnp.int32)
                key = jnp.where(bits < 0, bits ^ np.int32(0x7FFFFFFF), bits)
                hi_ref[kc, rows, :] = (key >> 16).astype(jnp.int16)
                lo_ref[kc, rows, :] = ((key & 0xFFFF) - 32768).astype(jnp.int16)

    for_range_by_two(nk, idx_step)

    one16, zero16, min16 = (jnp.full((), v, jnp.int16) for v in (1, 0, -32768))

    def select_keys(n):
        def reduce_keys(ref, fn, init, combine):
            acc = init
            for kc in range(n):
                for r in range(n_rg):
                    acc = combine(acc, fn(ref[kc, rg(r), :], kc * bk + r * RED_ROWS))
            return acc

        def total16(acc):
            return jnp.sum(acc.astype(jnp.int32), axis=0, keepdims=True)

        def count16(ref, pred):
            return total16(reduce_keys(ref, lambda x, off: jnp.where(pred(x), one16, zero16),
                                       jnp.zeros((RED_ROWS, bq), jnp.int16), jnp.add))

        def count16_top(pred):
            acc = jnp.zeros((RED_ROWS, bq), jnp.int16)
            for j in range(2 * n):
                acc = acc + jnp.where(pred(top_ref[j]), one16, zero16)
            return total16(acc)

        def count(pred):
            acc = reduce_keys(sc_ref, lambda s, off: jnp.where(pred(s, off), 1.0, 0.0),
                              jnp.zeros((RED_ROWS, bq), F32), jnp.add)
            return jnp.sum(acc, axis=0, keepdims=True)

        def key_min(fn):
            acc = reduce_keys(sc_ref, lambda s, off: fn(s), jnp.full((RED_ROWS, bq), jnp.inf, F32), jnp.minimum)
            return jnp.min(acc, axis=0, keepdims=True)

        def bisect16(count_ge, base):
            def body(it, u):
                trial = u | jnp.left_shift(jnp.int32(1), 15 - it)
                cand = (trial - 32768).astype(jnp.int16)
                return jnp.where(base + count_ge(cand) >= topk, trial, u)
            return lax.fori_loop(0, 16, body, jnp.zeros((1, bq), jnp.int32))

        hi_star = bisect16(lambda c: count16(hi_ref, lambda x: x >= c), 0) - 32768
        hi16 = hi_star.astype(jnp.int16)
        above = jnp.zeros((RED_ROWS, bq), jnp.int16)
        n_cand = jnp.zeros((RED_ROWS, bq), jnp.int16)
        for kc in range(n):
            top1 = jnp.full((RED_ROWS, bq), min16, jnp.int16)
            top2 = jnp.full((RED_ROWS, bq), min16, jnp.int16)
            for r in range(n_rg):
                hi = hi_ref[kc, rg(r), :]
                x = jnp.where(hi == hi16, lo_ref[kc, rg(r), :], min16)
                lo_ref[kc, rg(r), :] = x
                above = above + jnp.where(hi > hi16, one16, zero16)
                n_cand = n_cand + jnp.where(x > min16, one16, zero16)
                beats = x > top1
                lower = jnp.where(beats, top1, x)
                top1 = jnp.where(beats, x, top1)
                top2 = jnp.where(lower > top2, lower, top2)
            top_ref[2 * kc] = top1
            top_ref[2 * kc + 1] = top2
        above = total16(above)
        all_kept = jnp.min((total16(n_cand) == count16_top(lambda x: x > min16)).astype(jnp.int32)) == 1
        lo_star = lax.cond(all_kept,
                           lambda: bisect16(lambda c: count16_top(lambda x: x >= c), above),
                           lambda: bisect16(lambda c: count16(lo_ref, lambda x: x >= c), above))
        key = jnp.maximum(hi_star * 65536 + lo_star, KEY_NEG_F32_MAX)
        thr0 = lax.bitcast_convert_type(jnp.where(key < 0, key ^ np.int32(0x7FFFFFFF), key), F32)

        t = key_min(lambda s: jnp.where(s >= thr0, s, jnp.inf))
        c_gt = count(lambda s, off: s > t)

        def adv_body(st):
            t, c_gt, _ = st
            t_next = key_min(lambda s: jnp.where(s > t, s, jnp.inf))
            t = jnp.where(c_gt >= topk_f, t_next, t)
            c_gt = count(lambda s, off: s > t)
            return t, c_gt, jnp.max(c_gt)

        t, c_gt, _ = lax.while_loop(lambda st: st[2] >= topk_f, adv_body, (t, c_gt, jnp.max(c_gt)))

        c_ge = count(lambda s, off: s >= t)
        rem = topk_f - c_gt

        def tie_split():
            def body(it, p):
                trial = p | jnp.left_shift(jnp.int32(1), idx_bits - 1 - it)
                cnt = count(lambda s, off: (s == t) & (off + krow < trial))
                return jnp.where(cnt < rem, trial, p)
            return lax.fori_loop(0, idx_bits, body, jnp.zeros((1, bq), jnp.int32))

        def mask_pass(selected):
            for kc in range(n):
                for r in range(n_rg):
                    s = sc_ref[kc, rg(r), :]
                    sc_ref[kc, rg(r), :] = jnp.where(selected(s, kc * bk + r * RED_ROWS), 0.0, -jnp.inf)

        def mask_with_ties():
            last_eq = tie_split()
            mask_pass(lambda s, off: (s > t) | ((s == t) & (off + krow <= last_eq)))

        need_split = jnp.max(c_ge - c_gt - rem) > 0.0
        lax.cond(need_split, mask_with_ties, lambda: mask_pass(lambda s, off: s >= t))

    lax.switch(nk - 1, [functools.partial(select_keys, n) for n in range(1, max_chunks + 1)])

    m_ref[...] = jnp.full(m_ref.shape, -jnp.inf, F32)
    acc_ref[...] = jnp.zeros(acc_ref.shape, F32)

    def logits(kc, h):
        s = jnp.dot(k_ref[0, h // pair, chunk(kc), :], qp_ref[h], preferred_element_type=F32)
        mx = jnp.full((RED_ROWS, bq), -jnp.inf, F32)
        for r in range(n_rg):
            sb = s[rg(r)] + sc_ref[kc, rg(r), :]
            s_ref[h, rg(r), :] = sb
            mx = jnp.maximum(mx, sb)
        cm_ref[pl.ds(h, 1), :] = jnp.max(mx, axis=0, keepdims=True)

    def accumulate(kc, h):
        hrows = pl.ds(h * VT_ROWS, VT_ROWS)
        m_old = m_ref[pl.ds(h, 1), :]
        m_new = jnp.maximum(m_old, cm_ref[pl.ds(h, 1), :])
        m_use = jnp.where(m_new == -jnp.inf, 0.0, m_new)
        alpha = jnp.exp2(m_old - m_use)
        for r in range(n_rg):
            p_ref[h, rg(r), :] = jnp.exp2(s_ref[h, rg(r), :] - m_use).astype(BF16)
        pv = jnp.dot(vt_ref[0, kc, hrows, :], p_ref[h], preferred_element_type=F32)
        acc_ref[hrows, :] = alpha * acc_ref[hrows, :] + pv
        m_ref[pl.ds(h, 1), :] = m_new

    def for_heads(fn):
        for h in range(N_HEADS):
            fn(h)

    for_heads(lambda h: logits(jnp.int32(0), h))

    def att_step(kc):
        def both(h):
            accumulate(kc, h)
            logits(kc + 1, h)
        for_heads(both)

    def att_body(kc, carry):
        att_step(kc)
        return carry

    lax.fori_loop(0, nk - 1, att_body, 0)
    for_heads(lambda h: accumulate(nk - 1, h))

    outs = [acc_ref[h * VT_ROWS:h * VT_ROWS + HEAD_DIM, :] / acc_ref[h * VT_ROWS + HEAD_DIM:h * VT_ROWS + HEAD_DIM + 1, :]
            for h in range(N_HEADS)]
    o_ref[0] = jnp.concatenate(outs, axis=0).T.astype(BF16)


def _attention(qt, qit, wit, k, vt, ki):
    b, _, t = qt.shape
    bq, bk = ATT_BQ, ATT_BK
    topk = min(TOPK_MAX, t // 4)
    assert t % bk == 0 and t % bq == 0 and bq % LANES == 0 and bk % RED_ROWS == 0 and bk >= topk
    assert (bk // RED_ROWS) * (t // bk) < 2 ** 15
    idx_bits = max(1, int(np.ceil(np.log2(t))))
    qspec = lambda w: pl.BlockSpec((1, bq, w), lambda bi, i: (bi, i, 0))
    qtspec = lambda w: pl.BlockSpec((1, w, bq), lambda bi, i: (bi, 0, i))
    kspec = lambda w: pl.BlockSpec((1, t, w), lambda bi, i: (bi, 0, 0))
    return pl.pallas_call(
        functools.partial(_attn_kernel, bq=bq, bk=bk, topk=topk, idx_bits=idx_bits, max_chunks=t // bk),
        grid=(b, t // bq),
        in_specs=[
            qtspec(ATT_W), qtspec(IDX_W),
            pl.BlockSpec((1, IDX_HEADS, bq), lambda bi, i: (bi, 0, i)),
            pl.BlockSpec((1, ATT_W // LANES, t, LANES), lambda bi, i: (bi, 0, 0, 0)),
            pl.BlockSpec((1, t // bk, N_HEADS * VT_ROWS, bk), lambda bi, i: (bi, 0, 0, 0)),
            kspec(LANES),
        ],
        out_specs=qspec(ATT_W),
        out_shape=jax.ShapeDtypeStruct((b, t, ATT_W), BF16),
        scratch_shapes=[
            pltpu.VMEM((t // bk, bk, bq), F32),
            pltpu.VMEM((t // bk, bk, bq), jnp.int16),
            pltpu.VMEM((t // bk, bk, bq), jnp.int16),
            pltpu.VMEM((2 * (t // bk), RED_ROWS, bq), jnp.int16),
            pltpu.VMEM((N_HEADS, LANES, bq), BF16),
            pltpu.VMEM((IDX_HEADS, LANES, bq), BF16),
            pltpu.VMEM((N_HEADS, bk, bq), F32),
            pltpu.VMEM((N_HEADS, bk, bq), BF16),
            pltpu.VMEM((N_HEADS, bq), F32),
            pltpu.VMEM((N_HEADS, bq), F32),
            pltpu.VMEM((N_HEADS * VT_ROWS, bq), F32),
        ],
        compiler_params=_cparams(2),
        name="dsa_attention",
    )(qt, qit, wit, k, vt, ki)


def _merge_kernel(x_ref, g_ref, wg_ref, bg_ref, ya_ref, yb_ref, wpa_ref, wpb_ref, wo_ref, o_ref):
    x = x_ref[...]
    d = x.shape[1]
    h = _rms(x, g_ref[...]).astype(BF16)
    gates = jax.nn.sigmoid(jnp.dot(h, wg_ref[...], preferred_element_type=F32) + bg_ref[...])
    ya = jnp.dot(ya_ref[...], wpa_ref[...], preferred_element_type=F32)
    yb = jnp.dot(yb_ref[...], wpb_ref[...], preferred_element_type=F32)
    m = (gates[:, :d] * ya + gates[:, d:] * yb).astype(BF16)
    o_ref[...] = x + jnp.dot(m, wo_ref[...], preferred_element_type=F32)


def _merge(x, g, w_g, b_gate, ya, yb, w_pa, w_pb, w_out):
    n, d = x.shape
    tm = ROW_TM
    assert n % tm == 0
    full = lambda i: (0, 0)
    row = lambda i: (i, 0)
    return pl.pallas_call(
        _merge_kernel,
        grid=(n // tm,),
        in_specs=[
            pl.BlockSpec((tm, d), row),
            pl.BlockSpec((1, d), full),
            pl.BlockSpec((d, 2 * d), full),
            pl.BlockSpec((1, 2 * d), full),
            pl.BlockSpec((tm, A_HALF), row),
            pl.BlockSpec((tm, ATT_W), row),
            pl.BlockSpec((A_HALF, d), full),
            pl.BlockSpec((ATT_W, d), full),
            pl.BlockSpec((d, d), full),
        ],
        out_specs=pl.BlockSpec((tm, d), row),
        out_shape=jax.ShapeDtypeStruct((n, d), F32),
        compiler_params=_cparams(1),
        name="merge",
    )(x, g.reshape(1, d), w_g, b_gate.reshape(1, 2 * d), ya, yb, w_pa, w_pb, w_out)


def _rope_tables(seq):
    inv = ROPE_THETA ** (-jnp.arange(0, HEAD_DIM, 2, dtype=F32) / HEAD_DIM)
    ang = jnp.arange(seq, dtype=jnp.int32).astype(F32)[:, None] * inv[None, :]
    cos, sin = jnp.cos(ang), jnp.sin(ang)
    reps = LANES // HEAD_DIM
    cos2 = jnp.tile(jnp.concatenate([cos, cos], axis=1), (1, reps))
    sin2 = jnp.tile(jnp.concatenate([-sin, sin], axis=1), (1, reps))
    return cos2, sin2


def kernel(x, ffn1_norm, ffn1_w_gu, ffn1_w_down, mix_norm, w_in, b_gate, gmlp_ln_g, gmlp_ln_b, gmlp_w_s, gmlp_b_s, w_pa, w_pb, w_out, ffn2_norm, ffn2_w_gu, ffn2_w_down, final_norm):
    b, t, d = x.shape
    depth = ffn1_norm.shape[0]
    n = b * t
    cos2, sin2 = _rope_tables(t)
    c_uv = 2 * A_HALF
    c_qkv = c_uv + 3 * ATT_W
    c_qi = c_qkv + IDX_W
    c_ki = c_qi + IDX_DIM
    c_wi = c_ki + IDX_HEADS
    xf = x.reshape(n, d)
    for l in range(depth):
        w_in_l = w_in[l].astype(BF16)
        w_ki2 = jnp.concatenate([w_in_l[:, c_qi:c_ki]] * (LANES // IDX_DIM), axis=1)
        w_wi = jnp.pad(w_in_l[:, c_ki:c_wi], ((0, 0), (0, LANES - IDX_HEADS)))
        last = l == depth - 1
        xf = _ffn(xf, ffn1_norm[l], ffn1_w_gu[l].astype(BF16), ffn1_w_down[l].astype(BF16))
        ya = _gmlp(xf, mix_norm[l], w_in_l[:, :c_uv], gmlp_ln_g[l], gmlp_ln_b[l], gmlp_w_s[l], gmlp_b_s[l])
        qt, k, vt, qit, ki, wit = _prep(xf, mix_norm[l], w_in_l[:, c_uv:c_qkv], w_in_l[:, c_qkv:c_qi],
                                      w_ki2, w_wi, cos2, sin2, b, t)
        r3 = lambda a: a.reshape(b, t, a.shape[-1])
        yb = _attention(qt, qit, wit, k, vt, r3(ki)).reshape(n, ATT_W)
        xf = _merge(xf, mix_norm[l], w_in_l[:, c_wi:], b_gate[l], ya, yb,
                    w_pa[l].astype(BF16), w_pb[l].astype(BF16), w_out[l].astype(BF16))
        xf = _ffn(xf, ffn2_norm[l], ffn2_w_gu[l].astype(BF16), ffn2_w_down[l].astype(BF16),
                  final_g=final_norm if last else None)
    return xf.reshape(b, t, d)
```

```python
import functools

import jax
import jax.numpy as jnp
import numpy as np
from jax import lax
from jax.experimental import pallas as pl
from jax.experimental.pallas import tpu as pltpu

F32 = jnp.float32
BF16 = jnp.bfloat16

CHUNK = 128
A_GROUPS = 4
A_GROUP_CH = 128
A_HALF = A_GROUPS * A_GROUP_CH
N_HEADS = 8
HEAD_DIM = 64
ATT_W = N_HEADS * HEAD_DIM
IDX_HEADS = 8
IDX_DIM = 64
IDX_W = IDX_HEADS * IDX_DIM
TOPK_MAX = 256
ROPE_THETA = 10000.0
EPS = 1e-6
IDX_SCALE = (IDX_DIM ** -0.5) * (IDX_HEADS ** -0.5)
ATT_SCALE = HEAD_DIM ** -0.5
LOG2_E = float(np.log2(np.e))

LANES = 128
VMEM_LIMIT_BYTES = 56 * 1024 * 1024

FFN_TM = 512
ROW_TM = 512
ATT_BQ = 256
ATT_BK = ROW_TM
RED_ROWS = 64
IDX_SPLIT = 8
VT_ROWS = 80

KEY_NEG_F32_MAX = np.int32(-2139095040)


def _rms(x, g):
    return x * lax.rsqrt(jnp.mean(x * x, axis=-1, keepdims=True) + EPS) * g


def _cparams(n_axes):
    return pltpu.CompilerParams(
        dimension_semantics=("arbitrary",) * n_axes, vmem_limit_bytes=VMEM_LIMIT_BYTES)


def _ffn_kernel(x_ref, g_ref, wg_ref, wu_ref, wd_ref, *rest, final):
    if final:
        gf_ref, o_ref = rest
    else:
        (o_ref,) = rest
    x = x_ref[...]
    hn = _rms(x, g_ref[...]).astype(BF16)
    gate = jnp.dot(hn, wg_ref[...], preferred_element_type=F32)
    up = jnp.dot(hn, wu_ref[...], preferred_element_type=F32)
    a = (jax.nn.silu(gate) * up).astype(BF16)
    y = x + 0.5 * jnp.dot(a, wd_ref[...], preferred_element_type=F32)
    if final:
        y = _rms(y, gf_ref[...])
    o_ref[...] = y


def _ffn(x, g, w_gu, w_down, final_g=None):
    n, d = x.shape
    f = w_down.shape[0]
    tm = FFN_TM
    assert n % tm == 0 and f % LANES == 0
    final = final_g is not None
    resident = dict(pipeline_mode=pl.Buffered(1))
    in_specs = [
        pl.BlockSpec((tm, d), lambda i: (i, 0)),
        pl.BlockSpec((1, d), lambda i: (0, 0)),
        pl.BlockSpec((d, f), lambda i: (0, 0), **resident),
        pl.BlockSpec((d, f), lambda i: (0, 1), **resident),
        pl.BlockSpec((f, d), lambda i: (0, 0), **resident),
    ]
    args = [x, g.reshape(1, d), w_gu, w_gu, w_down]
    if final:
        in_specs.append(pl.BlockSpec((1, d), lambda i: (0, 0)))
        args.append(final_g.reshape(1, d))
    return pl.pallas_call(
        functools.partial(_ffn_kernel, final=final),
        grid=(n // tm,),
        in_specs=in_specs,
        out_specs=pl.BlockSpec((tm, d), lambda i: (i, 0)),
        out_shape=jax.ShapeDtypeStruct((n, d), F32),
        compiler_params=_cparams(1),
        name="ffn",
    )(*args)


def _gmlp_kernel(x_ref, g_ref, wuv_ref, lng_ref, lnb_ref, ws_ref, bs_ref, o_ref, *, tm):
    h = _rms(x_ref[...], g_ref[...]).astype(BF16)
    uv = jnp.dot(h, wuv_ref[...], preferred_element_type=F32)
    u = jax.nn.gelu(uv[:, :A_HALF])
    v = jax.nn.gelu(uv[:, A_HALF:])
    mu = jnp.mean(v, axis=-1, keepdims=True)
    var = jnp.mean(jnp.square(v - mu), axis=-1, keepdims=True)
    v = ((v - mu) * lax.rsqrt(var + EPS) * lng_ref[...] + lnb_ref[...]).astype(BF16)
    tri = (lax.broadcasted_iota(jnp.int32, (CHUNK, CHUNK), 1)
           <= lax.broadcasted_iota(jnp.int32, (CHUNK, CHUNK), 0))
    for g in range(A_GROUPS):
        ws = jnp.where(tri, ws_ref[g], 0.0).astype(BF16)
        cols = slice(g * A_GROUP_CH, (g + 1) * A_GROUP_CH)
        for c in range(tm // CHUNK):
            rows = slice(c * CHUNK, (c + 1) * CHUNK)
            mixed = jnp.dot(ws, v[rows, cols], preferred_element_type=F32) + bs_ref[g]
            o_ref[rows, cols] = (u[rows, cols] * mixed).astype(BF16)


def _gmlp(x, g, w_uv, ln_g, ln_b, w_s, b_s):
    n, d = x.shape
    tm = ROW_TM
    assert n % tm == 0 and tm % CHUNK == 0
    bs_b = jnp.broadcast_to(b_s[:, :, None], (A_GROUPS, CHUNK, A_GROUP_CH))
    return pl.pallas_call(
        functools.partial(_gmlp_kernel, tm=tm),
        grid=(n // tm,),
        in_specs=[
            pl.BlockSpec((tm, d), lambda i: (i, 0)),
            pl.BlockSpec((1, d), lambda i: (0, 0)),
            pl.BlockSpec((d, 2 * A_HALF), lambda i: (0, 0)),
            pl.BlockSpec((1, A_HALF), lambda i: (0, 0)),
            pl.BlockSpec((1, A_HALF), lambda i: (0, 0)),
            pl.BlockSpec((A_GROUPS, CHUNK, CHUNK), lambda i: (0, 0, 0)),
            pl.BlockSpec((A_GROUPS, CHUNK, A_GROUP_CH), lambda i: (0, 0, 0)),
        ],
        out_specs=pl.BlockSpec((tm, A_HALF), lambda i: (i, 0)),
        out_shape=jax.ShapeDtypeStruct((n, A_HALF), BF16),
        compiler_params=_cparams(1),
        name="gmlp",
    )(x, g.reshape(1, d), w_uv, ln_g.reshape(1, A_HALF), ln_b.reshape(1, A_HALF), w_s, bs_b)


def _rope(x, cos2, sin2):
    lane = lax.broadcasted_iota(jnp.int32, (x.shape[0], LANES), 1)
    first_half = (lane % HEAD_DIM) < (HEAD_DIM // 2)
    outs = []
    for c in range(x.shape[1] // LANES):
        xb = x[:, c * LANES:(c + 1) * LANES]
        partner = jnp.where(first_half,
                            pltpu.roll(xb, LANES - HEAD_DIM // 2, axis=1),
                            pltpu.roll(xb, HEAD_DIM // 2, axis=1))
        outs.append(xb * cos2 + partner * sin2)
    return outs[0] if len(outs) == 1 else jnp.concatenate(outs, axis=1)


def _prep_kernel(x_ref, g_ref, wqkv_ref, wqi_ref, wki_ref, wwi_ref, cos_ref, sin_ref,
                 qt_ref, k_ref, vt_ref, qit_ref, ki_ref, wit_ref):
    h = _rms(x_ref[...], g_ref[...]).astype(BF16)
    cos2, sin2 = cos_ref[...], sin_ref[...]
    qkv = jnp.dot(h, wqkv_ref[...], preferred_element_type=F32)
    qt_ref[0] = (_rope(qkv[:, :ATT_W], cos2, sin2) * (ATT_SCALE * LOG2_E)).T.astype(BF16)
    k = _rope(qkv[:, ATT_W:2 * ATT_W], cos2, sin2).astype(BF16)
    for c in range(ATT_W // LANES):
        k_ref[0, c] = k[:, c * LANES:(c + 1) * LANES]
    vt = qkv[:, 2 * ATT_W:].T
    tm = vt.shape[1]
    ones_rows = (lax.broadcasted_iota(jnp.int32, (VT_ROWS - HEAD_DIM, tm), 0) == 0).astype(F32)
    for hd in range(N_HEADS):
        vh = jnp.concatenate([vt[hd * HEAD_DIM:(hd + 1) * HEAD_DIM], ones_rows], axis=0)
        vt_ref[0, 0, hd * VT_ROWS:(hd + 1) * VT_ROWS, :] = vh.astype(BF16)
    qi = jnp.dot(h, wqi_ref[...], preferred_element_type=F32)
    qit_ref[0] = _rope(qi, cos2, sin2).T.astype(BF16)
    ki2 = jnp.dot(h, wki_ref[...], preferred_element_type=F32)
    ki_ref[...] = _rope(ki2, cos2, sin2).astype(BF16)
    wi = jnp.dot(h, wwi_ref[...], preferred_element_type=F32)
    wit_ref[0] = wi.T[:IDX_HEADS, :]


def _prep(x, g, w_qkv, w_qi, w_ki2, w_wi, cos2, sin2, batch, seq):
    n, d = x.shape
    tm = ROW_TM
    assert n % tm == 0 and seq % tm == 0 and tm == ATT_BK
    tpb = seq // tm
    full = lambda i: (0, 0)
    row = lambda i: (i, 0)
    pos = lambda i: (i % tpb, 0)
    return pl.pallas_call(
        _prep_kernel,
        grid=(n // tm,),
        in_specs=[
            pl.BlockSpec((tm, d), row),
            pl.BlockSpec((1, d), full),
            pl.BlockSpec((d, 3 * ATT_W), full),
            pl.BlockSpec((d, IDX_W), full),
            pl.BlockSpec((d, LANES), full),
            pl.BlockSpec((d, LANES), full),
            pl.BlockSpec((tm, LANES), pos),
            pl.BlockSpec((tm, LANES), pos),
        ],
        out_specs=[
            pl.BlockSpec((1, ATT_W, tm), lambda i: (i // tpb, 0, i % tpb)),
            pl.BlockSpec((1, ATT_W // LANES, tm, LANES), lambda i: (i // tpb, 0, i % tpb, 0)),
            pl.BlockSpec((1, 1, N_HEADS * VT_ROWS, tm), lambda i: (i // tpb, i % tpb, 0, 0)),
            pl.BlockSpec((1, IDX_W, tm), lambda i: (i // tpb, 0, i % tpb)),
            pl.BlockSpec((tm, LANES), row),
            pl.BlockSpec((1, IDX_HEADS, tm), lambda i: (i // tpb, 0, i % tpb)),
        ],
        out_shape=[
            jax.ShapeDtypeStruct((batch, ATT_W, seq), BF16),
            jax.ShapeDtypeStruct((batch, ATT_W // LANES, seq, LANES), BF16),
            jax.ShapeDtypeStruct((batch, tpb, N_HEADS * VT_ROWS, tm), BF16),
            jax.ShapeDtypeStruct((batch, IDX_W, seq), BF16),
            jax.ShapeDtypeStruct((n, LANES), BF16),
            jax.ShapeDtypeStruct((batch, IDX_HEADS, seq), F32),
        ],
        compiler_params=_cparams(1),
        name="attn_prep",
    )(x, g.reshape(1, d), w_qkv, w_qi, w_ki2, w_wi, cos2, sin2)


def _attn_kernel(qt_ref, qit_ref, wit_ref, k_ref, vt_ref, ki_ref, o_ref,
                 sc_ref, hi_ref, lo_ref, top_ref, qp_ref, qip_ref, s_ref, p_ref, cm_ref, m_ref, acc_ref,
                 *, bq, bk, topk, idx_bits, max_chunks):
    i = pl.program_id(1)
    nk = (i * bq + bq + bk - 1) // bk
    topk_f = float(topk)
    pair = LANES // HEAD_DIM
    n_rg = bk // RED_ROWS

    def rg(r):
        return slice(r * RED_ROWS, (r + 1) * RED_ROWS)

    def chunk(kc):
        return pl.ds(pl.multiple_of(kc * bk, bk), bk)

    def for_range_by_two(n, step):
        def body(j, carry):
            step(2 * j, False)
            step(2 * j + 1, True)
            return carry
        lax.fori_loop(0, lax.shift_right_logical(n, 1), body, 0)

        @pl.when((n & 1) == 1)
        def _():
            step(n - 1, True)

    row_head = lax.broadcasted_iota(jnp.int32, (LANES, bq), 0) // HEAD_DIM
    for h in range(N_HEADS):
        slab = slice((h // pair) * LANES, (h // pair + 1) * LANES)
        keep = row_head == (h % pair)
        qp_ref[h] = jnp.where(keep, qt_ref[0, slab, :], jnp.zeros((), BF16))
        qip_ref[h] = jnp.where(keep, qit_ref[0, slab, :], jnp.zeros((), BF16))

    qpos = i * bq + lax.broadcasted_iota(jnp.int32, (RED_ROWS, bq), 1)
    krow = lax.broadcasted_iota(jnp.int32, (RED_ROWS, bq), 0)
    wit = wit_ref[0] * IDX_SCALE

    def idx_step(kc, may_be_last):
        for half in range(IDX_SPLIT):
            hrows = bk // IDX_SPLIT
            base = half * hrows
            kib = ki_ref[0, pl.ds(pl.multiple_of(kc * bk + base, hrows), hrows), :]
            acc = jnp.zeros((hrows, bq), F32)
            for h in range(IDX_HEADS):
                l = jnp.dot(kib, qip_ref[h], preferred_element_type=F32)
                acc = acc + jnp.maximum(l, 0.0) * wit[h:h + 1, :]
            for r in range(hrows // RED_ROWS):
                rows = slice(base + r * RED_ROWS, base + (r + 1) * RED_ROWS)
                sc = acc[rg(r)]
                if may_be_last:
                    kpos = kc * bk + base + r * RED_ROWS + krow
                    sc = jnp.where(kpos <= qpos, sc, -jnp.inf)
                sc_ref[kc, rows, :] = sc
                bits = lax.bitcast_convert_type(sc, jnp.int32)
                key = jnp.where(bits < 0, bits ^ np.int32(0x7FFFFFFF), bits)
                hi_ref[kc, rows, :] = (key >> 16).astype(jnp.int16)
                lo_ref[kc, rows, :] = ((key & 0xFFFF) - 32768).astype(jnp.int16)

    for_range_by_two(nk, idx_step)

    one16, zero16, min16 = (jnp.full((), v, jnp.int16) for v in (1, 0, -32768))

    def select_keys(n):
        def reduce_keys(ref, fn, init, combine):
            acc = init
            for kc in range(n):
                for r in range(n_rg):
                    acc = combine(acc, fn(ref[kc, rg(r), :], kc * bk + r * RED_ROWS))
            return acc

        def total16(acc):
            return jnp.sum(acc.astype(jnp.int32), axis=0, keepdims=True)

        def count16(ref, pred):
            return total16(reduce_keys(ref, lambda x, off: jnp.where(pred(x), one16, zero16),
                                       jnp.zeros((RED_ROWS, bq), jnp.int16), jnp.add))

        def count16_top(pred):
            acc = jnp.zeros((RED_ROWS, bq), jnp.int16)
            for j in range(2 * n):
                acc = acc + jnp.where(pred(top_ref[j]), one16, zero16)
            return total16(acc)

        def count(pred):
            acc = reduce_keys(sc_ref, lambda s, off: jnp.where(pred(s, off), 1.0, 0.0),
                              jnp.zeros((RED_ROWS, bq), F32), jnp.add)
            return jnp.sum(acc, axis=0, keepdims=True)

        def key_min(fn):
            acc = reduce_keys(sc_ref, lambda s, off: fn(s), jnp.full((RED_ROWS, bq), jnp.inf, F32), jnp.minimum)
            return jnp.min(acc, axis=0, keepdims=True)

        def key_max(fn):
            acc = reduce_keys(sc_ref, lambda s, off: fn(s), jnp.full((RED_ROWS, bq), -jnp.inf, F32), jnp.maximum)
            return jnp.max(acc, axis=0, keepdims=True)

        def bisect16(count_ge, base):
            def body(it, u):
                trial = u | jnp.left_shift(jnp.int32(1), 15 - it)
                cand = (trial - 32768).astype(jnp.int16)
                return jnp.where(base + count_ge(cand) >= topk, trial, u)
            return lax.fori_loop(0, 16, body, jnp.zeros((1, bq), jnp.int32))

        hi_star = bisect16(lambda c: count16(hi_ref, lambda x: x >= c), 0) - 32768
        hi16 = hi_star.astype(jnp.int16)
        above = jnp.zeros((RED_ROWS, bq), jnp.int16)
        n_cand = jnp.zeros((RED_ROWS, bq), jnp.int16)
        for kc in range(n):
            top1 = jnp.full((RED_ROWS, bq), min16, jnp.int16)
            top2 = jnp.full((RED_ROWS, bq), min16, jnp.int16)
            for r in range(n_rg):
                hi = hi_ref[kc, rg(r), :]
                x = jnp.where(hi == hi16, lo_ref[kc, rg(r), :], min16)
                lo_ref[kc, rg(r), :] = x
                above = above + jnp.where(hi > hi16, one16, zero16)
                n_cand = n_cand + jnp.where(x > min16, one16, zero16)
                beats = x > top1
                lower = jnp.where(beats, top1, x)
                top1 = jnp.where(beats, x, top1)
                top2 = jnp.where(lower > top2, lower, top2)
            top_ref[2 * kc] = top1
            top_ref[2 * kc + 1] = top2
        above = total16(above)
        all_kept = jnp.min((total16(n_cand) == count16_top(lambda x: x > min16)).astype(jnp.int32)) == 1
        lo_star = lax.cond(all_kept,
                           lambda: bisect16(lambda c: count16_top(lambda x: x >= c), above),
                           lambda: bisect16(lambda c: count16(lo_ref, lambda x: x >= c), above))
        key = jnp.maximum(hi_star * 65536 + lo_star, KEY_NEG_F32_MAX)
        thr0 = lax.bitcast_convert_type(jnp.where(key < 0, key ^ np.int32(0x7FFFFFFF), key), F32)

        t = key_min(lambda s: jnp.where(s >= thr0, s, jnp.inf))
        c_ge = count(lambda s, off: s >= t)

        def down_body(st):
            t, c_ge, _ = st
            t_next = key_max(lambda s: jnp.where(s < t, s, -jnp.inf))
            step = (c_ge < topk_f) & (t_next > -jnp.inf)
            t = jnp.where(step, t_next, t)
            c_ge = count(lambda s, off: s >= t)
            return t, c_ge, jnp.max(step.astype(jnp.int32))

        t, c_ge, _ = lax.while_loop(lambda st: st[2] > 0, down_body,
                                    (t, c_ge, jnp.max((c_ge < topk_f).astype(jnp.int32))))
        c_gt = count(lambda s, off: s > t)

        def up_body(st):
            t, c_gt, c_ge, _ = st
            t_next = key_min(lambda s: jnp.where(s > t, s, jnp.inf))
            t = jnp.where(c_gt >= topk_f, t_next, t)
            c_gt = count(lambda s, off: s > t)
            c_ge = count(lambda s, off: s >= t)
            return t, c_gt, c_ge, jnp.max(c_gt)

        t, c_gt, c_ge, _ = lax.while_loop(lambda st: st[3] >= topk_f, up_body, (t, c_gt, c_ge, jnp.max(c_gt)))

        rem = topk_f - c_gt

        def tie_split():
            def body(it, p):
                trial = p | jnp.left_shift(jnp.int32(1), idx_bits - 1 - it)
                cnt = count(lambda s, off: (s == t) & (off + krow < trial))
                return jnp.where(cnt < rem, trial, p)
            return lax.fori_loop(0, idx_bits, body, jnp.zeros((1, bq), jnp.int32))

        def mask_pass(selected):
            for kc in range(n):
                for r in range(n_rg):
                    s = sc_ref[kc, rg(r), :]
                    sc_ref[kc, rg(r), :] = jnp.where(selected(s, kc * bk + r * RED_ROWS), 0.0, -jnp.inf)

        def mask_with_ties():
            last_eq = tie_split()
            mask_pass(lambda s, off: (s > t) | ((s == t) & (off + krow <= last_eq)))

        need_split = jnp.max(c_ge - c_gt - rem) > 0.0
        lax.cond(need_split, mask_with_ties, lambda: mask_pass(lambda s, off: s >= t))

    lax.switch(nk - 1, [functools.partial(select_keys, n) for n in range(1, max_chunks + 1)])

    m_ref[...] = jnp.full(m_ref.shape, -jnp.inf, F32)
    acc_ref[...] = jnp.zeros(acc_ref.shape, F32)

    def logits(kc, h):
        s = jnp.dot(k_ref[0, h // pair, chunk(kc), :], qp_ref[h], preferred_element_type=F32)
        mx = jnp.full((RED_ROWS, bq), -jnp.inf, F32)
        for r in range(n_rg):
            sb = s[rg(r)] + sc_ref[kc, rg(r), :]
            s_ref[h, rg(r), :] = sb
            mx = jnp.maximum(mx, sb)
        cm_ref[pl.ds(h, 1), :] = jnp.max(mx, axis=0, keepdims=True)

    def accumulate(kc, h):
        hrows = pl.ds(h * VT_ROWS, VT_ROWS)
        m_old = m_ref[pl.ds(h, 1), :]
        m_new = jnp.maximum(m_old, cm_ref[pl.ds(h, 1), :])
        m_use = jnp.where(m_new == -jnp.inf, 0.0, m_new)
        alpha = jnp.exp2(m_old - m_use)
        for r in range(n_rg):
            p_ref[h, rg(r), :] = jnp.exp2(s_ref[h, rg(r), :] - m_use).astype(BF16)
        pv = jnp.dot(vt_ref[0, kc, hrows, :], p_ref[h], preferred_element_type=F32)
        acc_ref[hrows, :] = alpha * acc_ref[hrows, :] + pv
        m_ref[pl.ds(h, 1), :] = m_new

    def for_heads(fn):
        for h in range(N_HEADS):
            fn(h)

    for_heads(lambda h: logits(jnp.int32(0), h))

    def att_step(kc):
        def both(h):
            accumulate(kc, h)
            logits(kc + 1, h)
        for_heads(both)

    def att_body(kc, carry):
        att_step(kc)
        return carry

    lax.fori_loop(0, nk - 1, att_body, 0)
    for_heads(lambda h: accumulate(nk - 1, h))

    outs = [acc_ref[h * VT_ROWS:h * VT_ROWS + HEAD_DIM, :] / acc_ref[h * VT_ROWS + HEAD_DIM:h * VT_ROWS + HEAD_DIM + 1, :]
            for h in range(N_HEADS)]
    o_ref[0] = jnp.concatenate(outs, axis=0).T.astype(BF16)


def _attention(qt, qit, wit, k, vt, ki):
    b, _, t = qt.shape
    bq, bk = ATT_BQ, ATT_BK
    topk = min(TOPK_MAX, t // 4)
    assert t % bk == 0 and t % bq == 0 and bq % LANES == 0 and bk % RED_ROWS == 0 and bk >= topk
    assert (bk // RED_ROWS) * (t // bk) < 2 ** 15
    idx_bits = max(1, int(np.ceil(np.log2(t))))
    qspec = lambda w: pl.BlockSpec((1, bq, w), lambda bi, i: (bi, i, 0))
    qtspec = lambda w: pl.BlockSpec((1, w, bq), lambda bi, i: (bi, 0, i))
    kspec = lambda w: pl.BlockSpec((1, t, w), lambda bi, i: (bi, 0, 0))
    return pl.pallas_call(
        functools.partial(_attn_kernel, bq=bq, bk=bk, topk=topk, idx_bits=idx_bits, max_chunks=t // bk),
        grid=(b, t // bq),
        in_specs=[
            qtspec(ATT_W), qtspec(IDX_W),
            pl.BlockSpec((1, IDX_HEADS, bq), lambda bi, i: (bi, 0, i)),
            pl.BlockSpec((1, ATT_W // LANES, t, LANES), lambda bi, i: (bi, 0, 0, 0)),
            pl.BlockSpec((1, t // bk, N_HEADS * VT_ROWS, bk), lambda bi, i: (bi, 0, 0, 0)),
            kspec(LANES),
        ],
        out_specs=qspec(ATT_W),
        out_shape=jax.ShapeDtypeStruct((b, t, ATT_W), BF16),
        scratch_shapes=[
            pltpu.VMEM((t // bk, bk, bq), F32),
            pltpu.VMEM((t // bk, bk, bq), jnp.int16),
            pltpu.VMEM((t // bk, bk, bq), jnp.int16),
            pltpu.VMEM((2 * (t // bk), RED_ROWS, bq), jnp.int16),
            pltpu.VMEM((N_HEADS, LANES, bq), BF16),
            pltpu.VMEM((IDX_HEADS, LANES, bq), BF16),
            pltpu.VMEM((N_HEADS, bk, bq), F32),
            pltpu.VMEM((N_HEADS, bk, bq), BF16),
            pltpu.VMEM((N_HEADS, bq), F32),
            pltpu.VMEM((N_HEADS, bq), F32),
            pltpu.VMEM((N_HEADS * VT_ROWS, bq), F32),
        ],
        compiler_params=_cparams(2),
        name="dsa_attention",
    )(qt, qit, wit, k, vt, ki)


def _merge_kernel(x_ref, g_ref, wg_ref, bg_ref, ya_ref, yb_ref, wpa_ref, wpb_ref, wo_ref, o_ref):
    x = x_ref[...]
    d = x.shape[1]
    h = _rms(x, g_ref[...]).astype(BF16)
    gates = jax.nn.sigmoid(jnp.dot(h, wg_ref[...], preferred_element_type=F32) + bg_ref[...])
    ya = jnp.dot(ya_ref[...], wpa_ref[...], preferred_element_type=F32)
    yb = jnp.dot(yb_ref[...], wpb_ref[...], preferred_element_type=F32)
    m = (gates[:, :d] * ya + gates[:, d:] * yb).astype(BF16)
    o_ref[...] = x + jnp.dot(m, wo_ref[...], preferred_element_type=F32)


def _merge(x, g, w_g, b_gate, ya, yb, w_pa, w_pb, w_out):
    n, d = x.shape
    tm = ROW_TM
    assert n % tm == 0
    full = lambda i: (0, 0)
    row = lambda i: (i, 0)
    return pl.pallas_call(
        _merge_kernel,
        grid=(n // tm,),
        in_specs=[
            pl.BlockSpec((tm, d), row),
            pl.BlockSpec((1, d), full),
            pl.BlockSpec((d, 2 * d), full),
            pl.BlockSpec((1, 2 * d), full),
            pl.BlockSpec((tm, A_HALF), row),
            pl.BlockSpec((tm, ATT_W), row),
            pl.BlockSpec((A_HALF, d), full),
            pl.BlockSpec((ATT_W, d), full),
            pl.BlockSpec((d, d), full),
        ],
        out_specs=pl.BlockSpec((tm, d), row),
        out_shape=jax.ShapeDtypeStruct((n, d), F32),
        compiler_params=_cparams(1),
        name="merge",
    )(x, g.reshape(1, d), w_g, b_gate.reshape(1, 2 * d), ya, yb, w_pa, w_pb, w_out)


def _rope_tables(seq):
    inv = ROPE_THETA ** (-jnp.arange(0, HEAD_DIM, 2, dtype=F32) / HEAD_DIM)
    ang = jnp.arange(seq, dtype=jnp.int32).astype(F32)[:, None] * inv[None, :]
    cos, sin = jnp.cos(ang), jnp.sin(ang)
    reps = LANES // HEAD_DIM
    cos2 = jnp.tile(jnp.concatenate([cos, cos], axis=1), (1, reps))
    sin2 = jnp.tile(jnp.concatenate([-sin, sin], axis=1), (1, reps))
    return cos2, sin2


def kernel(x, ffn1_norm, ffn1_w_gu, ffn1_w_down, mix_norm, w_in, b_gate, gmlp_ln_g, gmlp_ln_b, gmlp_w_s, gmlp_b_s, w_pa, w_pb, w_out, ffn2_norm, ffn2_w_gu, ffn2_w_down, final_norm):
    b, t, d = x.shape
    depth = ffn1_norm.shape[0]
    n = b * t
    cos2, sin2 = _rope_tables(t)
    c_uv = 2 * A_HALF
    c_qkv = c_uv + 3 * ATT_W
    c_qi = c_qkv + IDX_W
    c_ki = c_qi + IDX_DIM
    c_wi = c_ki + IDX_HEADS
    xf = x.reshape(n, d)
    for l in range(depth):
        w_in_l = w_in[l].astype(BF16)
        w_ki2 = jnp.concatenate([w_in_l[:, c_qi:c_ki]] * (LANES // IDX_DIM), axis=1)
        w_wi = jnp.pad(w_in_l[:, c_ki:c_wi], ((0, 0), (0, LANES - IDX_HEADS)))
        last = l == depth - 1
        xf = _ffn(xf, ffn1_norm[l], ffn1_w_gu[l].astype(BF16), ffn1_w_down[l].astype(BF16))
        ya = _gmlp(xf, mix_norm[l], w_in_l[:, :c_uv], gmlp_ln_g[l], gmlp_ln_b[l], gmlp_w_s[l], gmlp_b_s[l])
        qt, k, vt, qit, ki, wit = _prep(xf, mix_norm[l], w_in_l[:, c_uv:c_qkv], w_in_l[:, c_qkv:c_qi],
                                      w_ki2, w_wi, cos2, sin2, b, t)
        r3 = lambda a: a.reshape(b, t, a.shape[-1])
        yb = _attention(qt, qit, wit, k, vt, r3(ki)).reshape(n, ATT_W)
        xf = _merge(xf, mix_norm[l], w_in_l[:, c_wi:], b_gate[l], ya, yb,
                    w_pa[l].astype(BF16), w_pb[l].astype(BF16), w_out[l].astype(BF16))
        xf = _ffn(xf, ffn2_norm[l], ffn2_w_gu[l].astype(BF16), ffn2_w_down[l].astype(BF16),
                  final_g=final_norm if last else None)
    return xf.reshape(b, t, d)
```

```python
import functools

import jax
import jax.numpy as jnp
import numpy as np
from jax import lax
from jax.experimental import pallas as pl
from jax.experimental.pallas import tpu as pltpu

F32 = jnp.float32
BF16 = jnp.bfloat16

CHUNK = 128
A_GROUPS = 4
A_GROUP_CH = 128
A_HALF = A_GROUPS * A_GROUP_CH
N_HEADS = 8
HEAD_DIM = 64
ATT_W = N_HEADS * HEAD_DIM
IDX_HEADS = 8
IDX_DIM = 64
IDX_W = IDX_HEADS * IDX_DIM
TOPK_MAX = 256
ROPE_THETA = 10000.0
EPS = 1e-6
IDX_SCALE = (IDX_DIM ** -0.5) * (IDX_HEADS ** -0.5)
ATT_SCALE = HEAD_DIM ** -0.5
LOG2_E = float(np.log2(np.e))

LANES = 128
VMEM_LIMIT_BYTES = 56 * 1024 * 1024

FFN_TM = 512
ROW_TM = 512
ATT_BQ = 256
ATT_BK = ROW_TM
RED_ROWS = 64
IDX_SPLIT = 8
VT_ROWS = 80

KEY_NEG_F32_MAX = np.int32(-2139095040)


def _rms(x, g):
    return x * lax.rsqrt(jnp.mean(x * x, axis=-1, keepdims=True) + EPS) * g


def _cparams(n_axes):
    return pltpu.CompilerParams(
        dimension_semantics=("arbitrary",) * n_axes, vmem_limit_bytes=VMEM_LIMIT_BYTES)


def _ffn_kernel(x_ref, g_ref, wg_ref, wu_ref, wd_ref, *rest, final):
    if final:
        gf_ref, o_ref = rest
    else:
        (o_ref,) = rest
    x = x_ref[...]
    hn = _rms(x, g_ref[...]).astype(BF16)
    gate = jnp.dot(hn, wg_ref[...], preferred_element_type=F32)
    up = jnp.dot(hn, wu_ref[...], preferred_element_type=F32)
    a = (jax.nn.silu(gate) * up).astype(BF16)
    y = x + 0.5 * jnp.dot(a, wd_ref[...], preferred_element_type=F32)
    if final:
        y = _rms(y, gf_ref[...])
    o_ref[...] = y


def _ffn(x, g, w_gu, w_down, final_g=None):
    n, d = x.shape
    f = w_down.shape[0]
    tm = FFN_TM
    assert n % tm == 0 and f % LANES == 0
    final = final_g is not None
    resident = dict(pipeline_mode=pl.Buffered(1))
    in_specs = [
        pl.BlockSpec((tm, d), lambda i: (i, 0)),
        pl.BlockSpec((1, d), lambda i: (0, 0)),
        pl.BlockSpec((d, f), lambda i: (0, 0), **resident),
        pl.BlockSpec((d, f), lambda i: (0, 1), **resident),
        pl.BlockSpec((f, d), lambda i: (0, 0), **resident),
    ]
    args = [x, g.reshape(1, d), w_gu, w_gu, w_down]
    if final:
        in_specs.append(pl.BlockSpec((1, d), lambda i: (0, 0)))
        args.append(final_g.reshape(1, d))
    return pl.pallas_call(
        functools.partial(_ffn_kernel, final=final),
        grid=(n // tm,),
        in_specs=in_specs,
        out_specs=pl.BlockSpec((tm, d), lambda i: (i, 0)),
        out_shape=jax.ShapeDtypeStruct((n, d), F32),
        compiler_params=_cparams(1),
        name="ffn",
    )(*args)


def _gmlp_kernel(x_ref, g_ref, wuv_ref, lng_ref, lnb_ref, ws_ref, bs_ref, o_ref, *, tm):
    h = _rms(x_ref[...], g_ref[...]).astype(BF16)
    uv = jnp.dot(h, wuv_ref[...], preferred_element_type=F32)
    u = jax.nn.gelu(uv[:, :A_HALF])
    v = jax.nn.gelu(uv[:, A_HALF:])
    mu = jnp.mean(v, axis=-1, keepdims=True)
    var = jnp.mean(jnp.square(v - mu), axis=-1, keepdims=True)
    v = ((v - mu) * lax.rsqrt(var + EPS) * lng_ref[...] + lnb_ref[...]).astype(BF16)
    tri = (lax.broadcasted_iota(jnp.int32, (CHUNK, CHUNK), 1)
           <= lax.broadcasted_iota(jnp.int32, (CHUNK, CHUNK), 0))
    for g in range(A_GROUPS):
        ws = jnp.where(tri, ws_ref[g], 0.0).astype(BF16)
        cols = slice(g * A_GROUP_CH, (g + 1) * A_GROUP_CH)
        for c in range(tm // CHUNK):
            rows = slice(c * CHUNK, (c + 1) * CHUNK)
            mixed = jnp.dot(ws, v[rows, cols], preferred_element_type=F32) + bs_ref[g]
            o_ref[rows, cols] = (u[rows, cols] * mixed).astype(BF16)


def _gmlp(x, g, w_uv, ln_g, ln_b, w_s, b_s):
    n, d = x.shape
    tm = ROW_TM
    assert n % tm == 0 and tm % CHUNK == 0
    bs_b = jnp.broadcast_to(b_s[:, :, None], (A_GROUPS, CHUNK, A_GROUP_CH))
    return pl.pallas_call(
        functools.partial(_gmlp_kernel, tm=tm),
        grid=(n // tm,),
        in_specs=[
            pl.BlockSpec((tm, d), lambda i: (i, 0)),
            pl.BlockSpec((1, d), lambda i: (0, 0)),
            pl.BlockSpec((d, 2 * A_HALF), lambda i: (0, 0)),
            pl.BlockSpec((1, A_HALF), lambda i: (0, 0)),
            pl.BlockSpec((1, A_HALF), lambda i: (0, 0)),
            pl.BlockSpec((A_GROUPS, CHUNK, CHUNK), lambda i: (0, 0, 0)),
            pl.BlockSpec((A_GROUPS, CHUNK, A_GROUP_CH), lambda i: (0, 0, 0)),
        ],
        out_specs=pl.BlockSpec((tm, A_HALF), lambda i: (i, 0)),
        out_shape=jax.ShapeDtypeStruct((n, A_HALF), BF16),
        compiler_params=_cparams(1),
        name="gmlp",
    )(x, g.reshape(1, d), w_uv, ln_g.reshape(1, A_HALF), ln_b.reshape(1, A_HALF), w_s, bs_b)


def _rope(x, cos2, sin2):
    lane = lax.broadcasted_iota(jnp.int32, (x.shape[0], LANES), 1)
    first_half = (lane % HEAD_DIM) < (HEAD_DIM // 2)
    outs = []
    for c in range(x.shape[1] // LANES):
        xb = x[:, c * LANES:(c + 1) * LANES]
        partner = jnp.where(first_half,
                            pltpu.roll(xb, LANES - HEAD_DIM // 2, axis=1),
                            pltpu.roll(xb, HEAD_DIM // 2, axis=1))
        outs.append(xb * cos2 + partner * sin2)
    return outs[0] if len(outs) == 1 else jnp.concatenate(outs, axis=1)


def _prep_kernel(x_ref, g_ref, wqkv_ref, wqi_ref, wki_ref, wwi_ref, cos_ref, sin_ref,
                 qt_ref, k_ref, vt_ref, qit_ref, ki_ref, wit_ref):
    h = _rms(x_ref[...], g_ref[...]).astype(BF16)
    cos2, sin2 = cos_ref[...], sin_ref[...]
    qkv = jnp.dot(h, wqkv_ref[...], preferred_element_type=F32)
    qt_ref[0] = (_rope(qkv[:, :ATT_W], cos2, sin2) * (ATT_SCALE * LOG2_E)).T.astype(BF16)
    k = _rope(qkv[:, ATT_W:2 * ATT_W], cos2, sin2).astype(BF16)
    for c in range(ATT_W // LANES):
        k_ref[0, c] = k[:, c * LANES:(c + 1) * LANES]
    vt = qkv[:, 2 * ATT_W:].T
    tm = vt.shape[1]
    ones_rows = (lax.broadcasted_iota(jnp.int32, (VT_ROWS - HEAD_DIM, tm), 0) == 0).astype(F32)
    for hd in range(N_HEADS):
        vh = jnp.concatenate([vt[hd * HEAD_DIM:(hd + 1) * HEAD_DIM], ones_rows], axis=0)
        vt_ref[0, 0, hd * VT_ROWS:(hd + 1) * VT_ROWS, :] = vh.astype(BF16)
    qi = jnp.dot(h, wqi_ref[...], preferred_element_type=F32)
    qit_ref[0] = _rope(qi, cos2, sin2).T.astype(BF16)
    ki2 = jnp.dot(h, wki_ref[...], preferred_element_type=F32)
    ki_ref[...] = _rope(ki2, cos2, sin2).astype(BF16)
    wi = jnp.dot(h, wwi_ref[...], preferred_element_type=F32)
    wit_ref[0] = wi.T[:IDX_HEADS, :]


def _prep(x, g, w_qkv, w_qi, w_ki2, w_wi, cos2, sin2, batch, seq):
    n, d = x.shape
    tm = ROW_TM
    assert n % tm == 0 and seq % tm == 0 and tm == ATT_BK
    tpb = seq // tm
    full = lambda i: (0, 0)
    row = lambda i: (i, 0)
    pos = lambda i: (i % tpb, 0)
    return pl.pallas_call(
        _prep_kernel,
        grid=(n // tm,),
        in_specs=[
            pl.BlockSpec((tm, d), row),
            pl.BlockSpec((1, d), full),
            pl.BlockSpec((d, 3 * ATT_W), full),
            pl.BlockSpec((d, IDX_W), full),
            pl.BlockSpec((d, LANES), full),
            pl.BlockSpec((d, LANES), full),
            pl.BlockSpec((tm, LANES), pos),
            pl.BlockSpec((tm, LANES), pos),
        ],
        out_specs=[
            pl.BlockSpec((1, ATT_W, tm), lambda i: (i // tpb, 0, i % tpb)),
            pl.BlockSpec((1, ATT_W // LANES, tm, LANES), lambda i: (i // tpb, 0, i % tpb, 0)),
            pl.BlockSpec((1, 1, N_HEADS * VT_ROWS, tm), lambda i: (i // tpb, i % tpb, 0, 0)),
            pl.BlockSpec((1, IDX_W, tm), lambda i: (i // tpb, 0, i % tpb)),
            pl.BlockSpec((tm, LANES), row),
            pl.BlockSpec((1, IDX_HEADS, tm), lambda i: (i // tpb, 0, i % tpb)),
        ],
        out_shape=[
            jax.ShapeDtypeStruct((batch, ATT_W, seq), BF16),
            jax.ShapeDtypeStruct((batch, ATT_W // LANES, seq, LANES), BF16),
            jax.ShapeDtypeStruct((batch, tpb, N_HEADS * VT_ROWS, tm), BF16),
            jax.ShapeDtypeStruct((batch, IDX_W, seq), BF16),
            jax.ShapeDtypeStruct((n, LANES), BF16),
            jax.ShapeDtypeStruct((batch, IDX_HEADS, seq), F32),
        ],
        compiler_params=_cparams(1),
        name="attn_prep",
    )(x, g.reshape(1, d), w_qkv, w_qi, w_ki2, w_wi, cos2, sin2)


def _attn_kernel(qt_ref, qit_ref, wit_ref, k_ref, vt_ref, ki_ref, o_ref,
                 sc_ref, hi_ref, lo_ref, top_ref, qp_ref, qip_ref, s_ref, p_ref, cm_ref, m_ref, acc_ref,
                 *, bq, bk, topk, idx_bits, max_chunks):
    i = pl.program_id(1)
    nk = (i * bq + bq + bk - 1) // bk
    topk_f = float(topk)
    pair = LANES // HEAD_DIM
    n_rg = bk // RED_ROWS

    def rg(r):
        return slice(r * RED_ROWS, (r + 1) * RED_ROWS)

    def chunk(kc):
        return pl.ds(pl.multiple_of(kc * bk, bk), bk)

    def for_range_by_two(n, step):
        def body(j, carry):
            step(2 * j, False)
            step(2 * j + 1, True)
            return carry
        lax.fori_loop(0, lax.shift_right_logical(n, 1), body, 0)

        @pl.when((n & 1) == 1)
        def _():
            step(n - 1, True)

    row_head = lax.broadcasted_iota(jnp.int32, (LANES, bq), 0) // HEAD_DIM
    for h in range(N_HEADS):
        slab = slice((h // pair) * LANES, (h // pair + 1) * LANES)
        keep = row_head == (h % pair)
        qp_ref[h] = jnp.where(keep, qt_ref[0, slab, :], jnp.zeros((), BF16))
        qip_ref[h] = jnp.where(keep, qit_ref[0, slab, :], jnp.zeros((), BF16))

    qpos = i * bq + lax.broadcasted_iota(jnp.int32, (RED_ROWS, bq), 1)
    krow = lax.broadcasted_iota(jnp.int32, (RED_ROWS, bq), 0)
    wit = wit_ref[0] * IDX_SCALE

    def idx_step(kc, may_be_last):
        for half in range(IDX_SPLIT):
            hrows = bk // IDX_SPLIT
            base = half * hrows
            kib = ki_ref[0, pl.ds(pl.multiple_of(kc * bk + base, hrows), hrows), :]
            acc = jnp.zeros((hrows, bq), F32)
            for h in range(IDX_HEADS):
                l = jnp.dot(kib, qip_ref[h], preferred_element_type=F32)
                acc = acc + jnp.maximum(l, 0.0) * wit[h:h + 1, :]
            for r in range(hrows // RED_ROWS):
                rows = slice(base + r * RED_ROWS, base + (r + 1) * RED_ROWS)
                sc = acc[rg(r)]
                if may_be_last:
                    kpos = kc * bk + base + r * RED_ROWS + krow
                    sc = jnp.where(kpos <= qpos, sc, -jnp.inf)
                sc_ref[kc, rows, :] = sc
                bits = lax.bitcast_convert_type(sc, jnp.int32)
                key = jnp.where(bits < 0, bits ^ np.int32(0x7FFFFFFF), bits)
                hi_ref[kc, rows, :] = (key >> 16).astype(jnp.int16)
                lo_ref[kc, rows, :] = ((key & 0xFFFF) - 32768).astype(jnp.int16)

    for_range_by_two(nk, idx_step)

    one16, zero16, min16 = (jnp.full((), v, jnp.int16) for v in (1, 0, -32768))

    def select_keys(n):
        def reduce_keys(ref, fn, init, combine):
            acc = init
            for kc in range(n):
                for r in range(n_rg):
                    acc = combine(acc, fn(ref[kc, rg(r), :], kc * bk + r * RED_ROWS))
            return acc

        def total16(acc):
            return jnp.sum(acc.astype(jnp.int32), axis=0, keepdims=True)

        def count16(ref, pred):
            return total16(reduce_keys(ref, lambda x, off: jnp.where(pred(x), one16, zero16),
                                       jnp.zeros((RED_ROWS, bq), jnp.int16), jnp.add))

        def count16_top(pred):
            acc = jnp.zeros((RED_ROWS, bq), jnp.int16)
            for j in range(2 * n):
                acc = acc + jnp.where(pred(top_ref[j]), one16, zero16)
            return total16(acc)

        def count(pred):
            acc = reduce_keys(sc_ref, lambda s, off: jnp.where(pred(s, off), 1.0, 0.0),
                              jnp.zeros((RED_ROWS, bq), F32), jnp.add)
            return jnp.sum(acc, axis=0, keepdims=True)

        def key_min(fn):
            acc = reduce_keys(sc_ref, lambda s, off: fn(s), jnp.full((RED_ROWS, bq), jnp.inf, F32), jnp.minimum)
            return jnp.min(acc, axis=0, keepdims=True)

        def bisect16(count_ge, base):
            def body(it, u):
                trial = u | jnp.left_shift(jnp.int32(1), 15 - it)
                cand = (trial - 32768).astype(jnp.int16)
                return jnp.where(base + count_ge(cand) >= topk, trial, u)
            return lax.fori_loop(0, 16, body, jnp.zeros((1, bq), jnp.int32))

        hi_star = bisect16(lambda c: count16(hi_ref, lambda x: x >= c), 0) - 32768
        hi16 = hi_star.astype(jnp.int16)
        above = jnp.zeros((RED_ROWS, bq), jnp.int16)
        n_cand = jnp.zeros((RED_ROWS, bq), jnp.int16)
        for kc in range(n):
            top1 = jnp.full((RED_ROWS, bq), min16, jnp.int16)
            top2 = jnp.full((RED_ROWS, bq), min16, jnp.int16)
            for r in range(n_rg):
                hi = hi_ref[kc, rg(r), :]
                x = jnp.where(hi == hi16, lo_ref[kc, rg(r), :], min16)
                lo_ref[kc, rg(r), :] = x
                above = above + jnp.where(hi > hi16, one16, zero16)
                n_cand = n_cand + jnp.where(x > min16, one16, zero16)
                beats = x > top1
                lower = jnp.where(beats, top1, x)
                top1 = jnp.where(beats, x, top1)
                top2 = jnp.where(lower > top2, lower, top2)
            top_ref[2 * kc] = top1
            top_ref[2 * kc + 1] = top2
        above = total16(above)
        all_kept = jnp.min((total16(n_cand) == count16_top(lambda x: x > min16)).astype(jnp.int32)) == 1
        lo_star = lax.cond(all_kept,
                           lambda: bisect16(lambda c: count16_top(lambda x: x >= c), above),
                           lambda: bisect16(lambda c: count16(lo_ref, lambda x: x >= c), above))
        key = jnp.maximum(hi_star * 65536 + lo_star, KEY_NEG_F32_MAX)
        thr0 = lax.bitcast_convert_type(jnp.where(key < 0, key ^ np.int32(0x7FFFFFFF), key), F32)

        t = key_min(lambda s: jnp.where(s >= thr0, s, jnp.inf))
        return t, count(lambda s, off: s >= t), count(lambda s, off: s > t)

    t, c_ge, c_gt = lax.switch(nk - 1, [functools.partial(select_keys, n) for n in range(1, max_chunks + 1)])

    def reduce_scores(fn, init, combine):
        def body(kc, acc):
            for r in range(n_rg):
                acc = combine(acc, fn(sc_ref[kc, rg(r), :], kc * bk + r * RED_ROWS))
            return acc
        return lax.fori_loop(0, nk, body, init)

    def count(pred):
        acc = reduce_scores(lambda s, off: jnp.where(pred(s, off), 1.0, 0.0), jnp.zeros((RED_ROWS, bq), F32), jnp.add)
        return jnp.sum(acc, axis=0, keepdims=True)

    def key_min(fn):
        acc = reduce_scores(lambda s, off: fn(s), jnp.full((RED_ROWS, bq), jnp.inf, F32), jnp.minimum)
        return jnp.min(acc, axis=0, keepdims=True)

    def key_max(fn):
        acc = reduce_scores(lambda s, off: fn(s), jnp.full((RED_ROWS, bq), -jnp.inf, F32), jnp.maximum)
        return jnp.max(acc, axis=0, keepdims=True)

    def down_body(st):
        t, c_ge, c_gt, _ = st
        t_next = key_max(lambda s: jnp.where(s < t, s, -jnp.inf))
        step = (c_ge < topk_f) & (t_next > -jnp.inf)
        t = jnp.where(step, t_next, t)
        return t, count(lambda s, off: s >= t), count(lambda s, off: s > t), jnp.max(step.astype(jnp.int32))

    t, c_ge, c_gt, _ = lax.while_loop(lambda st: st[3] > 0, down_body,
                                      (t, c_ge, c_gt, jnp.max((c_ge < topk_f).astype(jnp.int32))))

    def up_body(st):
        t, c_ge, c_gt, _ = st
        t_next = key_min(lambda s: jnp.where(s > t, s, jnp.inf))
        t = jnp.where(c_gt >= topk_f, t_next, t)
        c_gt = count(lambda s, off: s > t)
        return t, count(lambda s, off: s >= t), c_gt, jnp.max(c_gt)

    t, c_ge, c_gt, _ = lax.while_loop(lambda st: st[3] >= topk_f, up_body, (t, c_ge, c_gt, jnp.max(c_gt)))

    rem = topk_f - c_gt

    def tie_split():
        def body(it, p):
            trial = p | jnp.left_shift(jnp.int32(1), idx_bits - 1 - it)
            cnt = count(lambda s, off: (s == t) & (off + krow < trial))
            return jnp.where(cnt < rem, trial, p)
        return lax.fori_loop(0, idx_bits, body, jnp.zeros((1, bq), jnp.int32))

    def mask_pass(selected):
        def body(kc, carry):
            for r in range(n_rg):
                s = sc_ref[kc, rg(r), :]
                sc_ref[kc, rg(r), :] = jnp.where(selected(s, kc * bk + r * RED_ROWS), 0.0, -jnp.inf)
            return carry
        lax.fori_loop(0, nk, body, 0)

    def mask_with_ties():
        last_eq = tie_split()
        mask_pass(lambda s, off: (s > t) | ((s == t) & (off + krow <= last_eq)))

    need_split = jnp.max(c_ge - c_gt - rem) > 0.0
    lax.cond(need_split, mask_with_ties, lambda: mask_pass(lambda s, off: s >= t))

    m_ref[...] = jnp.full(m_ref.shape, -jnp.inf, F32)
    acc_ref[...] = jnp.zeros(acc_ref.shape, F32)

    def logits(kc, h):
        s = jnp.dot(k_ref[0, h // pair, chunk(kc), :], qp_ref[h], preferred_element_type=F32)
        mx = jnp.full((RED_ROWS, bq), -jnp.inf, F32)
        for r in range(n_rg):
            sb = s[rg(r)] + sc_ref[kc, rg(r), :]
            s_ref[h, rg(r), :] = sb
            mx = jnp.maximum(mx, sb)
        cm_ref[pl.ds(h, 1), :] = jnp.max(mx, axis=0, keepdims=True)

    def accumulate(kc, h):
        hrows = pl.ds(h * VT_ROWS, VT_ROWS)
        m_old = m_ref[pl.ds(h, 1), :]
        m_new = jnp.maximum(m_old, cm_ref[pl.ds(h, 1), :])
        m_use = jnp.where(m_new == -jnp.inf, 0.0, m_new)
        alpha = jnp.exp2(m_old - m_use)
        for r in range(n_rg):
            p_ref[h, rg(r), :] = jnp.exp2(s_ref[h, rg(r), :] - m_use).astype(BF16)
        pv = jnp.dot(vt_ref[0, kc, hrows, :], p_ref[h], preferred_element_type=F32)
        acc_ref[hrows, :] = alpha * acc_ref[hrows, :] + pv
        m_ref[pl.ds(h, 1), :] = m_new

    def for_heads(fn):
        for h in range(N_HEADS):
            fn(h)

    for_heads(lambda h: logits(jnp.int32(0), h))

    def att_step(kc):
        def both(h):
            accumulate(kc, h)
            logits(kc + 1, h)
        for_heads(both)

    def att_body(kc, carry):
        att_step(kc)
        return carry

    lax.fori_loop(0, nk - 1, att_body, 0)
    for_heads(lambda h: accumulate(nk - 1, h))

    outs = [acc_ref[h * VT_ROWS:h * VT_ROWS + HEAD_DIM, :] / acc_ref[h * VT_ROWS + HEAD_DIM:h * VT_ROWS + HEAD_DIM + 1, :]
            for h in range(N_HEADS)]
    o_ref[0] = jnp.concatenate(outs, axis=0).T.astype(BF16)


def _attention(qt, qit, wit, k, vt, ki):
    b, _, t = qt.shape
    bq, bk = ATT_BQ, ATT_BK
    topk = min(TOPK_MAX, t // 4)
    assert t % bk == 0 and t % bq == 0 and bq % LANES == 0 and bk % RED_ROWS == 0 and bk >= topk
    assert (bk // RED_ROWS) * (t // bk) < 2 ** 15
    idx_bits = max(1, int(np.ceil(np.log2(t))))
    qspec = lambda w: pl.BlockSpec((1, bq, w), lambda bi, i: (bi, i, 0))
    qtspec = lambda w: pl.BlockSpec((1, w, bq), lambda bi, i: (bi, 0, i))
    kspec = lambda w: pl.BlockSpec((1, t, w), lambda bi, i: (bi, 0, 0))
    return pl.pallas_call(
        functools.partial(_attn_kernel, bq=bq, bk=bk, topk=topk, idx_bits=idx_bits, max_chunks=t // bk),
        grid=(b, t // bq),
        in_specs=[
            qtspec(ATT_W), qtspec(IDX_W),
            pl.BlockSpec((1, IDX_HEADS, bq), lambda bi, i: (bi, 0, i)),
            pl.BlockSpec((1, ATT_W // LANES, t, LANES), lambda bi, i: (bi, 0, 0, 0)),
            pl.BlockSpec((1, t // bk, N_HEADS * VT_ROWS, bk), lambda bi, i: (bi, 0, 0, 0)),
            kspec(LANES),
        ],
        out_specs=qspec(ATT_W),
        out_shape=jax.ShapeDtypeStruct((b, t, ATT_W), BF16),
        scratch_shapes=[
            pltpu.VMEM((t // bk, bk, bq), F32),
            pltpu.VMEM((t // bk, bk, bq), jnp.int16),
            pltpu.VMEM((t // bk, bk, bq), jnp.int16),
            pltpu.VMEM((2 * (t // bk), RED_ROWS, bq), jnp.int16),
            pltpu.VMEM((N_HEADS, LANES, bq), BF16),
            pltpu.VMEM((IDX_HEADS, LANES, bq), BF16),
            pltpu.VMEM((N_HEADS, bk, bq), F32),
            pltpu.VMEM((N_HEADS, bk, bq), BF16),
            pltpu.VMEM((N_HEADS, bq), F32),
            pltpu.VMEM((N_HEADS, bq), F32),
            pltpu.VMEM((N_HEADS * VT_ROWS, bq), F32),
        ],
        compiler_params=_cparams(2),
        name="dsa_attention",
    )(qt, qit, wit, k, vt, ki)


def _merge_kernel(x_ref, g_ref, wg_ref, bg_ref, ya_ref, yb_ref, wpa_ref, wpb_ref, wo_ref, o_ref):
    x = x_ref[...]
    d = x.shape[1]
    h = _rms(x, g_ref[...]).astype(BF16)
    gates = jax.nn.sigmoid(jnp.dot(h, wg_ref[...], preferred_element_type=F32) + bg_ref[...])
    ya = jnp.dot(ya_ref[...], wpa_ref[...], preferred_element_type=F32)
    yb = jnp.dot(yb_ref[...], wpb_ref[...], preferred_element_type=F32)
    m = (gates[:, :d] * ya + gates[:, d:] * yb).astype(BF16)
    o_ref[...] = x + jnp.dot(m, wo_ref[...], preferred_element_type=F32)


def _merge(x, g, w_g, b_gate, ya, yb, w_pa, w_pb, w_out):
    n, d = x.shape
    tm = ROW_TM
    assert n % tm == 0
    full = lambda i: (0, 0)
    row = lambda i: (i, 0)
    return pl.pallas_call(
        _merge_kernel,
        grid=(n // tm,),
        in_specs=[
            pl.BlockSpec((tm, d), row),
            pl.BlockSpec((1, d), full),
            pl.BlockSpec((d, 2 * d), full),
            pl.BlockSpec((1, 2 * d), full),
            pl.BlockSpec((tm, A_HALF), row),
            pl.BlockSpec((tm, ATT_W), row),
            pl.BlockSpec((A_HALF, d), full),
            pl.BlockSpec((ATT_W, d), full),
            pl.BlockSpec((d, d), full),
        ],
        out_specs=pl.BlockSpec((tm, d), row),
        out_shape=jax.ShapeDtypeStruct((n, d), F32),
        compiler_params=_cparams(1),
        name="merge",
    )(x, g.reshape(1, d), w_g, b_gate.reshape(1, 2 * d), ya, yb, w_pa, w_pb, w_out)


def _rope_tables(seq):
    inv = ROPE_THETA ** (-jnp.arange(0, HEAD_DIM, 2, dtype=F32) / HEAD_DIM)
    ang = jnp.arange(seq, dtype=jnp.int32).astype(F32)[:, None] * inv[None, :]
    cos, sin = jnp.cos(ang), jnp.sin(ang)
    reps = LANES // HEAD_DIM
    cos2 = jnp.tile(jnp.concatenate([cos, cos], axis=1), (1, reps))
    sin2 = jnp.tile(jnp.concatenate([-sin, sin], axis=1), (1, reps))
    return cos2, sin2


def kernel(x, ffn1_norm, ffn1_w_gu, ffn1_w_down, mix_norm, w_in, b_gate, gmlp_ln_g, gmlp_ln_b, gmlp_w_s, gmlp_b_s, w_pa, w_pb, w_out, ffn2_norm, ffn2_w_gu, ffn2_w_down, final_norm):
    b, t, d = x.shape
    depth = ffn1_norm.shape[0]
    n = b * t
    cos2, sin2 = _rope_tables(t)
    c_uv = 2 * A_HALF
    c_qkv = c_uv + 3 * ATT_W
    c_qi = c_qkv + IDX_W
    c_ki = c_qi + IDX_DIM
    c_wi = c_ki + IDX_HEADS
    xf = x.reshape(n, d)
    for l in range(depth):
        w_in_l = w_in[l].astype(BF16)
        w_ki2 = jnp.concatenate([w_in_l[:, c_qi:c_ki]] * (LANES // IDX_DIM), axis=1)
        w_wi = jnp.pad(w_in_l[:, c_ki:c_wi], ((0, 0), (0, LANES - IDX_HEADS)))
        last = l == depth - 1
        xf = _ffn(xf, ffn1_norm[l], ffn1_w_gu[l].astype(BF16), ffn1_w_down[l].astype(BF16))
        ya = _gmlp(xf, mix_norm[l], w_in_l[:, :c_uv], gmlp_ln_g[l], gmlp_ln_b[l], gmlp_w_s[l], gmlp_b_s[l])
        qt, k, vt, qit, ki, wit = _prep(xf, mix_norm[l], w_in_l[:, c_uv:c_qkv], w_in_l[:, c_qkv:c_qi],
                                      w_ki2, w_wi, cos2, sin2, b, t)
        r3 = lambda a: a.reshape(b, t, a.shape[-1])
        yb = _attention(qt, qit, wit, k, vt, r3(ki)).reshape(n, ATT_W)
        xf = _merge(xf, mix_norm[l], w_in_l[:, c_wi:], b_gate[l], ya, yb,
                    w_pa[l].astype(BF16), w_pb[l].astype(BF16), w_out[l].astype(BF16))
        xf = _ffn(xf, ffn2_norm[l], ffn2_w_gu[l].astype(BF16), ffn2_w_down[l].astype(BF16),
                  final_g=final_norm if last else None)
    return xf.reshape(b, t, d)
```

```python
import functools

import jax
import jax.numpy as jnp
import numpy as np
from jax import lax
from jax.experimental import pallas as pl
from jax.experimental.pallas import tpu as pltpu

F32 = jnp.float32
BF16 = jnp.bfloat16

CHUNK = 128
A_GROUPS = 4
A_GROUP_CH = 128
A_HALF = A_GROUPS * A_GROUP_CH
N_HEADS = 8
HEAD_DIM = 64
ATT_W = N_HEADS * HEAD_DIM
IDX_HEADS = 8
IDX_DIM = 64
IDX_W = IDX_HEADS * IDX_DIM
TOPK_MAX = 256
ROPE_THETA = 10000.0
EPS = 1e-6
IDX_SCALE = (IDX_DIM ** -0.5) * (IDX_HEADS ** -0.5)
ATT_SCALE = HEAD_DIM ** -0.5
LOG2_E = float(np.log2(np.e))

LANES = 128
VMEM_LIMIT_BYTES = 56 * 1024 * 1024

FFN_TM = 512
ROW_TM = 512
ATT_BQ = 256
ATT_BK = ROW_TM
RED_ROWS = 32
IDX_SPLIT = 8
VT_ROWS = 80

KEY_NEG_F32_MAX = np.int32(-2139095040)


def _rms(x, g):
    return x * lax.rsqrt(jnp.mean(x * x, axis=-1, keepdims=True) + EPS) * g


def _cparams(n_axes):
    return pltpu.CompilerParams(
        dimension_semantics=("arbitrary",) * n_axes, vmem_limit_bytes=VMEM_LIMIT_BYTES)


def _ffn_kernel(x_ref, g_ref, wg_ref, wu_ref, wd_ref, *rest, final):
    if final:
        gf_ref, o_ref = rest
    else:
        (o_ref,) = rest
    x = x_ref[...]
    hn = _rms(x, g_ref[...]).astype(BF16)
    gate = jnp.dot(hn, wg_ref[...], preferred_element_type=F32)
    up = jnp.dot(hn, wu_ref[...], preferred_element_type=F32)
    a = (jax.nn.silu(gate) * up).astype(BF16)
    y = x + 0.5 * jnp.dot(a, wd_ref[...], preferred_element_type=F32)
    if final:
        y = _rms(y, gf_ref[...])
    o_ref[...] = y


def _ffn(x, g, w_gu, w_down, final_g=None):
    n, d = x.shape
    f = w_down.shape[0]
    tm = FFN_TM
    assert n % tm == 0 and f % LANES == 0
    final = final_g is not None
    resident = dict(pipeline_mode=pl.Buffered(1))
    in_specs = [
        pl.BlockSpec((tm, d), lambda i: (i, 0)),
        pl.BlockSpec((1, d), lambda i: (0, 0)),
        pl.BlockSpec((d, f), lambda i: (0, 0), **resident),
        pl.BlockSpec((d, f), lambda i: (0, 1), **resident),
        pl.BlockSpec((f, d), lambda i: (0, 0), **resident),
    ]
    args = [x, g.reshape(1, d), w_gu, w_gu, w_down]
    if final:
        in_specs.append(pl.BlockSpec((1, d), lambda i: (0, 0)))
        args.append(final_g.reshape(1, d))
    return pl.pallas_call(
        functools.partial(_ffn_kernel, final=final),
        grid=(n // tm,),
        in_specs=in_specs,
        out_specs=pl.BlockSpec((tm, d), lambda i: (i, 0)),
        out_shape=jax.ShapeDtypeStruct((n, d), F32),
        compiler_params=_cparams(1),
        name="ffn",
    )(*args)


def _gmlp_kernel(x_ref, g_ref, wuv_ref, lng_ref, lnb_ref, ws_ref, bs_ref, o_ref, *, tm):
    h = _rms(x_ref[...], g_ref[...]).astype(BF16)
    uv = jnp.dot(h, wuv_ref[...], preferred_element_type=F32)
    u = jax.nn.gelu(uv[:, :A_HALF])
    v = jax.nn.gelu(uv[:, A_HALF:])
    mu = jnp.mean(v, axis=-1, keepdims=True)
    var = jnp.mean(jnp.square(v - mu), axis=-1, keepdims=True)
    v = ((v - mu) * lax.rsqrt(var + EPS) * lng_ref[...] + lnb_ref[...]).astype(BF16)
    tri = (lax.broadcasted_iota(jnp.int32, (CHUNK, CHUNK), 1)
           <= lax.broadcasted_iota(jnp.int32, (CHUNK, CHUNK), 0))
    for g in range(A_GROUPS):
        ws = jnp.where(tri, ws_ref[g], 0.0).astype(BF16)
        cols = slice(g * A_GROUP_CH, (g + 1) * A_GROUP_CH)
        for c in range(tm // CHUNK):
            rows = slice(c * CHUNK, (c + 1) * CHUNK)
            mixed = jnp.dot(ws, v[rows, cols], preferred_element_type=F32) + bs_ref[g]
            o_ref[rows, cols] = (u[rows, cols] * mixed).astype(BF16)


def _gmlp(x, g, w_uv, ln_g, ln_b, w_s, b_s):
    n, d = x.shape
    tm = ROW_TM
    assert n % tm == 0 and tm % CHUNK == 0
    bs_b = jnp.broadcast_to(b_s[:, :, None], (A_GROUPS, CHUNK, A_GROUP_CH))
    return pl.pallas_call(
        functools.partial(_gmlp_kernel, tm=tm),
        grid=(n // tm,),
        in_specs=[
            pl.BlockSpec((tm, d), lambda i: (i, 0)),
            pl.BlockSpec((1, d), lambda i: (0, 0)),
            pl.BlockSpec((d, 2 * A_HALF), lambda i: (0, 0)),
            pl.BlockSpec((1, A_HALF), lambda i: (0, 0)),
            pl.BlockSpec((1, A_HALF), lambda i: (0, 0)),
            pl.BlockSpec((A_GROUPS, CHUNK, CHUNK), lambda i: (0, 0, 0)),
            pl.BlockSpec((A_GROUPS, CHUNK, A_GROUP_CH), lambda i: (0, 0, 0)),
        ],
        out_specs=pl.BlockSpec((tm, A_HALF), lambda i: (i, 0)),
        out_shape=jax.ShapeDtypeStruct((n, A_HALF), BF16),
        compiler_params=_cparams(1),
        name="gmlp",
    )(x, g.reshape(1, d), w_uv, ln_g.reshape(1, A_HALF), ln_b.reshape(1, A_HALF), w_s, bs_b)


def _rope(x, cos2, sin2):
    lane = lax.broadcasted_iota(jnp.int32, (x.shape[0], LANES), 1)
    first_half = (lane % HEAD_DIM) < (HEAD_DIM // 2)
    outs = []
    for c in range(x.shape[1] // LANES):
        xb = x[:, c * LANES:(c + 1) * LANES]
        partner = jnp.where(first_half,
                            pltpu.roll(xb, LANES - HEAD_DIM // 2, axis=1),
                            pltpu.roll(xb, HEAD_DIM // 2, axis=1))
        outs.append(xb * cos2 + partner * sin2)
    return outs[0] if len(outs) == 1 else jnp.concatenate(outs, axis=1)


def _prep_kernel(x_ref, g_ref, wqkv_ref, wqi_ref, wki_ref, wwi_ref, cos_ref, sin_ref,
                 qt_ref, k_ref, vt_ref, qit_ref, ki_ref, wit_ref):
    h = _rms(x_ref[...], g_ref[...]).astype(BF16)
    cos2, sin2 = cos_ref[...], sin_ref[...]
    qkv = jnp.dot(h, wqkv_ref[...], preferred_element_type=F32)
    qt_ref[0] = (_rope(qkv[:, :ATT_W], cos2, sin2) * (ATT_SCALE * LOG2_E)).T.astype(BF16)
    k = _rope(qkv[:, ATT_W:2 * ATT_W], cos2, sin2).astype(BF16)
    for c in range(ATT_W // LANES):
        k_ref[0, c] = k[:, c * LANES:(c + 1) * LANES]
    vt = qkv[:, 2 * ATT_W:].T
    tm = vt.shape[1]
    ones_rows = (lax.broadcasted_iota(jnp.int32, (VT_ROWS - HEAD_DIM, tm), 0) == 0).astype(F32)
    for hd in range(N_HEADS):
        vh = jnp.concatenate([vt[hd * HEAD_DIM:(hd + 1) * HEAD_DIM], ones_rows], axis=0)
        vt_ref[0, 0, hd * VT_ROWS:(hd + 1) * VT_ROWS, :] = vh.astype(BF16)
    qi = jnp.dot(h, wqi_ref[...], preferred_element_type=F32)
    qit_ref[0] = _rope(qi, cos2, sin2).T.astype(BF16)
    ki2 = jnp.dot(h, wki_ref[...], preferred_element_type=F32)
    ki_ref[...] = _rope(ki2, cos2, sin2).astype(BF16)
    wi = jnp.dot(h, wwi_ref[...], preferred_element_type=F32)
    wit_ref[0] = wi.T[:IDX_HEADS, :]


def _prep(x, g, w_qkv, w_qi, w_ki2, w_wi, cos2, sin2, batch, seq):
    n, d = x.shape
    tm = ROW_TM
    assert n % tm == 0 and seq % tm == 0 and tm == ATT_BK
    tpb = seq // tm
    full = lambda i: (0, 0)
    row = lambda i: (i, 0)
    pos = lambda i: (i % tpb, 0)
    return pl.pallas_call(
        _prep_kernel,
        grid=(n // tm,),
        in_specs=[
            pl.BlockSpec((tm, d), row),
            pl.BlockSpec((1, d), full),
            pl.BlockSpec((d, 3 * ATT_W), full),
            pl.BlockSpec((d, IDX_W), full),
            pl.BlockSpec((d, LANES), full),
            pl.BlockSpec((d, LANES), full),
            pl.BlockSpec((tm, LANES), pos),
            pl.BlockSpec((tm, LANES), pos),
        ],
        out_specs=[
            pl.BlockSpec((1, ATT_W, tm), lambda i: (i // tpb, 0, i % tpb)),
            pl.BlockSpec((1, ATT_W // LANES, tm, LANES), lambda i: (i // tpb, 0, i % tpb, 0)),
            pl.BlockSpec((1, 1, N_HEADS * VT_ROWS, tm), lambda i: (i // tpb, i % tpb, 0, 0)),
            pl.BlockSpec((1, IDX_W, tm), lambda i: (i // tpb, 0, i % tpb)),
            pl.BlockSpec((tm, LANES), row),
            pl.BlockSpec((1, IDX_HEADS, tm), lambda i: (i // tpb, 0, i % tpb)),
        ],
        out_shape=[
            jax.ShapeDtypeStruct((batch, ATT_W, seq), BF16),
            jax.ShapeDtypeStruct((batch, ATT_W // LANES, seq, LANES), BF16),
            jax.ShapeDtypeStruct((batch, tpb, N_HEADS * VT_ROWS, tm), BF16),
            jax.ShapeDtypeStruct((batch, IDX_W, seq), BF16),
            jax.ShapeDtypeStruct((n, LANES), BF16),
            jax.ShapeDtypeStruct((batch, IDX_HEADS, seq), F32),
        ],
        compiler_params=_cparams(1),
        name="attn_prep",
    )(x, g.reshape(1, d), w_qkv, w_qi, w_ki2, w_wi, cos2, sin2)


def _attn_kernel(qt_ref, qit_ref, wit_ref, k_ref, vt_ref, ki_ref, o_ref,
                 sc_ref, hi_ref, lo_ref, top_ref, qp_ref, qip_ref, s_ref, p_ref, cm_ref, m_ref, acc_ref,
                 *, bq, bk, topk, idx_bits, max_chunks):
    i = pl.program_id(1)
    nk = (i * bq + bq + bk - 1) // bk
    topk_f = float(topk)
    pair = LANES // HEAD_DIM
    n_rg = bk // RED_ROWS

    def rg(r):
        return slice(r * RED_ROWS, (r + 1) * RED_ROWS)

    def chunk(kc):
        return pl.ds(pl.multiple_of(kc * bk, bk), bk)

    def for_range_by_two(n, step):
        def body(j, carry):
            step(2 * j, False)
            step(2 * j + 1, True)
            return carry
        lax.fori_loop(0, lax.shift_right_logical(n, 1), body, 0)

        @pl.when((n & 1) == 1)
        def _():
            step(n - 1, True)

    row_head = lax.broadcasted_iota(jnp.int32, (LANES, bq), 0) // HEAD_DIM
    for h in range(N_HEADS):
        slab = slice((h // pair) * LANES, (h // pair + 1) * LANES)
        keep = row_head == (h % pair)
        qp_ref[h] = jnp.where(keep, qt_ref[0, slab, :], jnp.zeros((), BF16))
        qip_ref[h] = jnp.where(keep, qit_ref[0, slab, :], jnp.zeros((), BF16))

    qpos = i * bq + lax.broadcasted_iota(jnp.int32, (RED_ROWS, bq), 1)
    krow = lax.broadcasted_iota(jnp.int32, (RED_ROWS, bq), 0)
    wit = wit_ref[0] * IDX_SCALE

    def idx_step(kc, may_be_last):
        for half in range(IDX_SPLIT):
            hrows = bk // IDX_SPLIT
            base = half * hrows
            kib = ki_ref[0, pl.ds(pl.multiple_of(kc * bk + base, hrows), hrows), :]
            acc = jnp.zeros((hrows, bq), F32)
            for h in range(IDX_HEADS):
                l = jnp.dot(kib, qip_ref[h], preferred_element_type=F32)
                acc = acc + jnp.maximum(l, 0.0) * wit[h:h + 1, :]
            for r in range(hrows // RED_ROWS):
                rows = slice(base + r * RED_ROWS, base + (r + 1) * RED_ROWS)
                sc = acc[rg(r)]
                if may_be_last:
                    kpos = kc * bk + base + r * RED_ROWS + krow
                    sc = jnp.where(kpos <= qpos, sc, -jnp.inf)
                sc_ref[kc, rows, :] = sc
                bits = lax.bitcast_convert_type(sc, jnp.int32)
                key = jnp.where(bits < 0, bits ^ np.int32(0x7FFFFFFF), bits)
                hi_ref[kc, rows, :] = (key >> 16).astype(jnp.int16)
                lo_ref[kc, rows, :] = ((key & 0xFFFF) - 32768).astype(jnp.int16)

    for_range_by_two(nk, idx_step)

    one16, zero16, min16 = (jnp.full((), v, jnp.int16) for v in (1, 0, -32768))

    def select_keys(n):
        def reduce_keys(ref, fn, init, combine):
            acc = init
            for kc in range(n):
                for r in range(n_rg):
                    acc = combine(acc, fn(ref[kc, rg(r), :], kc * bk + r * RED_ROWS))
            return acc

        def total16(acc):
            return jnp.sum(acc.astype(jnp.int32), axis=0, keepdims=True)

        def count16(ref, pred):
            return total16(reduce_keys(ref, lambda x, off: jnp.where(pred(x), one16, zero16),
                                       jnp.zeros((RED_ROWS, bq), jnp.int16), jnp.add))

        def count16_top(pred):
            acc = jnp.zeros((RED_ROWS, bq), jnp.int16)
            for j in range(2 * n):
                acc = acc + jnp.where(pred(top_ref[j]), one16, zero16)
            return total16(acc)

        def count(pred):
            acc = reduce_keys(sc_ref, lambda s, off: jnp.where(pred(s, off), 1.0, 0.0),
                              jnp.zeros((RED_ROWS, bq), F32), jnp.add)
            return jnp.sum(acc, axis=0, keepdims=True)

        def key_min(fn):
            acc = reduce_keys(sc_ref, lambda s, off: fn(s), jnp.full((RED_ROWS, bq), jnp.inf, F32), jnp.minimum)
            return jnp.min(acc, axis=0, keepdims=True)

        def bisect16(count_ge, base):
            def body(it, u):
                trial = u | jnp.left_shift(jnp.int32(1), 15 - it)
                cand = (trial - 32768).astype(jnp.int16)
                return jnp.where(base + count_ge(cand) >= topk, trial, u)
            return lax.fori_loop(0, 16, body, jnp.zeros((1, bq), jnp.int32))

        hi_star = bisect16(lambda c: count16(hi_ref, lambda x: x >= c), 0) - 32768
        hi16 = hi_star.astype(jnp.int16)
        above = jnp.zeros((RED_ROWS, bq), jnp.int16)
        n_cand = jnp.zeros((RED_ROWS, bq), jnp.int16)
        for kc in range(n):
            top1 = jnp.full((RED_ROWS, bq), min16, jnp.int16)
            top2 = jnp.full((RED_ROWS, bq), min16, jnp.int16)
            for r in range(n_rg):
                hi = hi_ref[kc, rg(r), :]
                x = jnp.where(hi == hi16, lo_ref[kc, rg(r), :], min16)
                lo_ref[kc, rg(r), :] = x
                above = above + jnp.where(hi > hi16, one16, zero16)
                n_cand = n_cand + jnp.where(x > min16, one16, zero16)
                beats = x > top1
                lower = jnp.where(beats, top1, x)
                top1 = jnp.where(beats, x, top1)
                top2 = jnp.where(lower > top2, lower, top2)
            top_ref[2 * kc] = top1
            top_ref[2 * kc + 1] = top2
        above = total16(above)
        all_kept = jnp.min((total16(n_cand) == count16_top(lambda x: x > min16)).astype(jnp.int32)) == 1
        lo_star = lax.cond(all_kept,
                           lambda: bisect16(lambda c: count16_top(lambda x: x >= c), above),
                           lambda: bisect16(lambda c: count16(lo_ref, lambda x: x >= c), above))
        key = jnp.maximum(hi_star * 65536 + lo_star, KEY_NEG_F32_MAX)
        thr0 = lax.bitcast_convert_type(jnp.where(key < 0, key ^ np.int32(0x7FFFFFFF), key), F32)

        t = key_min(lambda s: jnp.where(s >= thr0, s, jnp.inf))
        return t, count(lambda s, off: s >= t), count(lambda s, off: s > t)

    t, c_ge, c_gt = lax.switch(nk - 1, [functools.partial(select_keys, n) for n in range(1, max_chunks + 1)])

    def reduce_scores(fn, init, combine):
        def body(kc, acc):
            for r in range(n_rg):
                acc = combine(acc, fn(sc_ref[kc, rg(r), :], kc * bk + r * RED_ROWS))
            return acc
        return lax.fori_loop(0, nk, body, init)

    def count(pred):
        acc = reduce_scores(lambda s, off: jnp.where(pred(s, off), 1.0, 0.0), jnp.zeros((RED_ROWS, bq), F32), jnp.add)
        return jnp.sum(acc, axis=0, keepdims=True)

    def key_min(fn):
        acc = reduce_scores(lambda s, off: fn(s), jnp.full((RED_ROWS, bq), jnp.inf, F32), jnp.minimum)
        return jnp.min(acc, axis=0, keepdims=True)

    def key_max(fn):
        acc = reduce_scores(lambda s, off: fn(s), jnp.full((RED_ROWS, bq), -jnp.inf, F32), jnp.maximum)
        return jnp.max(acc, axis=0, keepdims=True)

    def down_body(st):
        t, c_ge, c_gt, _ = st
        t_next = key_max(lambda s: jnp.where(s < t, s, -jnp.inf))
        step = (c_ge < topk_f) & (t_next > -jnp.inf)
        t = jnp.where(step, t_next, t)
        return t, count(lambda s, off: s >= t), count(lambda s, off: s > t), jnp.max(step.astype(jnp.int32))

    t, c_ge, c_gt, _ = lax.while_loop(lambda st: st[3] > 0, down_body,
                                      (t, c_ge, c_gt, jnp.max((c_ge < topk_f).astype(jnp.int32))))

    def up_body(st):
        t, c_ge, c_gt, _ = st
        t_next = key_min(lambda s: jnp.where(s > t, s, jnp.inf))
        t = jnp.where(c_gt >= topk_f, t_next, t)
        c_gt = count(lambda s, off: s > t)
        return t, count(lambda s, off: s >= t), c_gt, jnp.max(c_gt)

    t, c_ge, c_gt, _ = lax.while_loop(lambda st: st[3] >= topk_f, up_body, (t, c_ge, c_gt, jnp.max(c_gt)))

    rem = topk_f - c_gt

    def tie_split():
        def body(it, p):
            trial = p | jnp.left_shift(jnp.int32(1), idx_bits - 1 - it)
            cnt = count(lambda s, off: (s == t) & (off + krow < trial))
            return jnp.where(cnt < rem, trial, p)
        return lax.fori_loop(0, idx_bits, body, jnp.zeros((1, bq), jnp.int32))

    def mask_pass(selected):
        def body(kc, carry):
            for r in range(n_rg):
                s = sc_ref[kc, rg(r), :]
                sc_ref[kc, rg(r), :] = jnp.where(selected(s, kc * bk + r * RED_ROWS), 0.0, -jnp.inf)
            return carry
        lax.fori_loop(0, nk, body, 0)

    def mask_with_ties():
        last_eq = tie_split()
        mask_pass(lambda s, off: (s > t) | ((s == t) & (off + krow <= last_eq)))

    need_split = jnp.max(c_ge - c_gt - rem) > 0.0
    lax.cond(need_split, mask_with_ties, lambda: mask_pass(lambda s, off: s >= t))

    m_ref[...] = jnp.full(m_ref.shape, -jnp.inf, F32)
    acc_ref[...] = jnp.zeros(acc_ref.shape, F32)

    def logits(kc, h):
        s = jnp.dot(k_ref[0, h // pair, chunk(kc), :], qp_ref[h], preferred_element_type=F32)
        mx = jnp.full((RED_ROWS, bq), -jnp.inf, F32)
        for r in range(n_rg):
            sb = s[rg(r)] + sc_ref[kc, rg(r), :]
            s_ref[h, rg(r), :] = sb
            mx = jnp.maximum(mx, sb)
        cm_ref[pl.ds(h, 1), :] = jnp.max(mx, axis=0, keepdims=True)

    def accumulate(kc, h):
        hrows = pl.ds(h * VT_ROWS, VT_ROWS)
        m_old = m_ref[pl.ds(h, 1), :]
        m_new = jnp.maximum(m_old, cm_ref[pl.ds(h, 1), :])
        m_use = jnp.where(m_new == -jnp.inf, 0.0, m_new)
        alpha = jnp.exp2(m_old - m_use)
        for r in range(n_rg):
            p_ref[h, rg(r), :] = jnp.exp2(s_ref[h, rg(r), :] - m_use).astype(BF16)
        pv = jnp.dot(vt_ref[0, kc, hrows, :], p_ref[h], preferred_element_type=F32)
        acc_ref[hrows, :] = alpha * acc_ref[hrows, :] + pv
        m_ref[pl.ds(h, 1), :] = m_new

    def for_heads(fn):
        for h in range(N_HEADS):
            fn(h)

    for_heads(lambda h: logits(jnp.int32(0), h))

    def att_step(kc):
        def both(h):
            accumulate(kc, h)
            logits(kc + 1, h)
        for_heads(both)

    def att_body(kc, carry):
        att_step(kc)
        return carry

    lax.fori_loop(0, nk - 1, att_body, 0)
    for_heads(lambda h: accumulate(nk - 1, h))

    outs = [acc_ref[h * VT_ROWS:h * VT_ROWS + HEAD_DIM, :] / acc_ref[h * VT_ROWS + HEAD_DIM:h * VT_ROWS + HEAD_DIM + 1, :]
            for h in range(N_HEADS)]
    o_ref[0] = jnp.concatenate(outs, axis=0).T.astype(BF16)


def _attention(qt, qit, wit, k, vt, ki):
    b, _, t = qt.shape
    bq, bk = ATT_BQ, ATT_BK
    topk = min(TOPK_MAX, t // 4)
    assert t % bk == 0 and t % bq == 0 and bq % LANES == 0 and bk % RED_ROWS == 0 and bk >= topk
    assert (bk // RED_ROWS) * (t // bk) < 2 ** 15
    idx_bits = max(1, int(np.ceil(np.log2(t))))
    qspec = lambda w: pl.BlockSpec((1, bq, w), lambda bi, i: (bi, i, 0))
    qtspec = lambda w: pl.BlockSpec((1, w, bq), lambda bi, i: (bi, 0, i))
    kspec = lambda w: pl.BlockSpec((1, t, w), lambda bi, i: (bi, 0, 0))
    return pl.pallas_call(
        functools.partial(_attn_kernel, bq=bq, bk=bk, topk=topk, idx_bits=idx_bits, max_chunks=t // bk),
        grid=(b, t // bq),
        in_specs=[
            qtspec(ATT_W), qtspec(IDX_W),
            pl.BlockSpec((1, IDX_HEADS, bq), lambda bi, i: (bi, 0, i)),
            pl.BlockSpec((1, ATT_W // LANES, t, LANES), lambda bi, i: (bi, 0, 0, 0)),
            pl.BlockSpec((1, t // bk, N_HEADS * VT_ROWS, bk), lambda bi, i: (bi, 0, 0, 0)),
            kspec(LANES),
        ],
        out_specs=qspec(ATT_W),
        out_shape=jax.ShapeDtypeStruct((b, t, ATT_W), BF16),
        scratch_shapes=[
            pltpu.VMEM((t // bk, bk, bq), F32),
            pltpu.VMEM((t // bk, bk, bq), jnp.int16),
            pltpu.VMEM((t // bk, bk, bq), jnp.int16),
            pltpu.VMEM((2 * (t // bk), RED_ROWS, bq), jnp.int16),
            pltpu.VMEM((N_HEADS, LANES, bq), BF16),
            pltpu.VMEM((IDX_HEADS, LANES, bq), BF16),
            pltpu.VMEM((N_HEADS, bk, bq), F32),
            pltpu.VMEM((N_HEADS, bk, bq), BF16),
            pltpu.VMEM((N_HEADS, bq), F32),
            pltpu.VMEM((N_HEADS, bq), F32),
            pltpu.VMEM((N_HEADS * VT_ROWS, bq), F32),
        ],
        compiler_params=_cparams(2),
        name="dsa_attention",
    )(qt, qit, wit, k, vt, ki)


def _merge_kernel(x_ref, g_ref, wg_ref, bg_ref, ya_ref, yb_ref, wpa_ref, wpb_ref, wo_ref, o_ref):
    x = x_ref[...]
    d = x.shape[1]
    h = _rms(x, g_ref[...]).astype(BF16)
    gates = jax.nn.sigmoid(jnp.dot(h, wg_ref[...], preferred_element_type=F32) + bg_ref[...])
    ya = jnp.dot(ya_ref[...], wpa_ref[...], preferred_element_type=F32)
    yb = jnp.dot(yb_ref[...], wpb_ref[...], preferred_element_type=F32)
    m = (gates[:, :d] * ya + gates[:, d:] * yb).astype(BF16)
    o_ref[...] = x + jnp.dot(m, wo_ref[...], preferred_element_type=F32)


def _merge(x, g, w_g, b_gate, ya, yb, w_pa, w_pb, w_out):
    n, d = x.shape
    tm = ROW_TM
    assert n % tm == 0
    full = lambda i: (0, 0)
    row = lambda i: (i, 0)
    return pl.pallas_call(
        _merge_kernel,
        grid=(n // tm,),
        in_specs=[
            pl.BlockSpec((tm, d), row),
            pl.BlockSpec((1, d), full),
            pl.BlockSpec((d, 2 * d), full),
            pl.BlockSpec((1, 2 * d), full),
            pl.BlockSpec((tm, A_HALF), row),
            pl.BlockSpec((tm, ATT_W), row),
            pl.BlockSpec((A_HALF, d), full),
            pl.BlockSpec((ATT_W, d), full),
            pl.BlockSpec((d, d), full),
        ],
        out_specs=pl.BlockSpec((tm, d), row),
        out_shape=jax.ShapeDtypeStruct((n, d), F32),
        compiler_params=_cparams(1),
        name="merge",
    )(x, g.reshape(1, d), w_g, b_gate.reshape(1, 2 * d), ya, yb, w_pa, w_pb, w_out)


def _rope_tables(seq):
    inv = ROPE_THETA ** (-jnp.arange(0, HEAD_DIM, 2, dtype=F32) / HEAD_DIM)
    ang = jnp.arange(seq, dtype=jnp.int32).astype(F32)[:, None] * inv[None, :]
    cos, sin = jnp.cos(ang), jnp.sin(ang)
    reps = LANES // HEAD_DIM
    cos2 = jnp.tile(jnp.concatenate([cos, cos], axis=1), (1, reps))
    sin2 = jnp.tile(jnp.concatenate([-sin, sin], axis=1), (1, reps))
    return cos2, sin2


def kernel(x, ffn1_norm, ffn1_w_gu, ffn1_w_down, mix_norm, w_in, b_gate, gmlp_ln_g, gmlp_ln_b, gmlp_w_s, gmlp_b_s, w_pa, w_pb, w_out, ffn2_norm, ffn2_w_gu, ffn2_w_down, final_norm):
    b, t, d = x.shape
    depth = ffn1_norm.shape[0]
    n = b * t
    cos2, sin2 = _rope_tables(t)
    c_uv = 2 * A_HALF
    c_qkv = c_uv + 3 * ATT_W
    c_qi = c_qkv + IDX_W
    c_ki = c_qi + IDX_DIM
    c_wi = c_ki + IDX_HEADS
    xf = x.reshape(n, d)
    for l in range(depth):
        w_in_l = w_in[l].astype(BF16)
        w_ki2 = jnp.concatenate([w_in_l[:, c_qi:c_ki]] * (LANES // IDX_DIM), axis=1)
        w_wi = jnp.pad(w_in_l[:, c_ki:c_wi], ((0, 0), (0, LANES - IDX_HEADS)))
        last = l == depth - 1
        xf = _ffn(xf, ffn1_norm[l], ffn1_w_gu[l].astype(BF16), ffn1_w_down[l].astype(BF16))
        ya = _gmlp(xf, mix_norm[l], w_in_l[:, :c_uv], gmlp_ln_g[l], gmlp_ln_b[l], gmlp_w_s[l], gmlp_b_s[l])
        qt, k, vt, qit, ki, wit = _prep(xf, mix_norm[l], w_in_l[:, c_uv:c_qkv], w_in_l[:, c_qkv:c_qi],
                                      w_ki2, w_wi, cos2, sin2, b, t)
        r3 = lambda a: a.reshape(b, t, a.shape[-1])
        yb = _attention(qt, qit, wit, k, vt, r3(ki)).reshape(n, ATT_W)
        xf = _merge(xf, mix_norm[l], w_in_l[:, c_wi:], b_gate[l], ya, yb,
                    w_pa[l].astype(BF16), w_pb[l].astype(BF16), w_out[l].astype(BF16))
        xf = _ffn(xf, ffn2_norm[l], ffn2_w_gu[l].astype(BF16), ffn2_w_down[l].astype(BF16),
                  final_g=final_norm if last else None)
    return xf.reshape(b, t, d)
```
